```python
import math
import jax, jax.numpy as jnp
from jax import lax
import numpy as np

D_MODEL = 1024
BATCH = 8
SEQ = 4096
DEPTH = 4

CHUNK = 64
POOL_WIDTH = D_MODEL // 2
POOL_WINDOWS = (2, 4, 8, 16)
N_POOL_GROUPS = len(POOL_WINDOWS)
POOL_GROUP = POOL_WIDTH // N_POOL_GROUPS
ATTN_WIDTH = D_MODEL - POOL_WIDTH
HEAD_DIM = 64
N_HEADS = ATTN_WIDTH // HEAD_DIM
ROPE_DIM = HEAD_DIM // 4
ROPE_THETA = 500000.0
IDX_HEADS = 8
IDX_DIM = 64
IDX_ROPE_DIM = IDX_DIM // 4
TOPK_MAX = 256
Q_BLOCK = 64
D_FF = 4 * D_MODEL
EPS = 1e-6

SPLITS = (POOL_WIDTH, ATTN_WIDTH, ATTN_WIDTH, ATTN_WIDTH, IDX_HEADS * IDX_DIM, IDX_DIM, IDX_HEADS)
IN_WIDTH = sum(SPLITS)
SPLIT_POINTS = tuple(int(v) for v in np.cumsum(SPLITS)[:-1])

kernel_name = "hymba_pool_dsa_sandwich_trunk"


def rms_norm(x, g):
    xf = x.astype(jnp.float32)
    y = xf * lax.rsqrt(jnp.mean(xf * xf, axis=-1, keepdims=True) + EPS)
    return (y * g.astype(jnp.float32)).astype(x.dtype)


def rope_partial(x, pos, rot_dim):
    half = rot_dim // 2
    inv = ROPE_THETA ** (-jnp.arange(half, dtype=jnp.float32) / half)
    ang = pos.astype(jnp.float32)[..., None] * inv
    cos = jnp.cos(ang)[:, :, None, :]
    sin = jnp.sin(ang)[:, :, None, :]
    x1 = x[..., :half].astype(jnp.float32)
    x2 = x[..., half:rot_dim].astype(jnp.float32)
    rot = jnp.concatenate([x1 * cos - x2 * sin, x2 * cos + x1 * sin], axis=-1)
    return jnp.concatenate([rot.astype(x.dtype), x[..., rot_dim:]], axis=-1)


def pool_mixer(u, w_pool, pool_scale):
    B, S, _ = u.shape
    ug = u.reshape(B, S, N_POOL_GROUPS, POOL_GROUP)
    cs = jnp.cumsum(ug.astype(jnp.float32), axis=1)
    t = jnp.arange(S)
    outs = []
    for g, w in enumerate(POOL_WINDOWS):
        c = cs[:, :, g]
        lagged = jnp.pad(c, ((0, 0), (w, 0), (0, 0)))[:, :S]
        cnt = jnp.minimum(t + 1, w).astype(jnp.float32)[None, :, None]
        outs.append((c - lagged) / cnt - ug[:, :, g].astype(jnp.float32))
    d = jnp.stack(outs, axis=2).astype(u.dtype)
    y = jnp.einsum('bsgc,gcd->bsgd', d, w_pool)
    return y.reshape(B, S, POOL_WIDTH) * pool_scale


def dsa_attention(q, k, v, iq, ik, iw, topk):
    B, S = q.shape[0], q.shape[1]
    n_blocks = S // Q_BLOCK
    key_chunk = jnp.arange(S) // CHUNK
    ik32 = ik.astype(jnp.float32)

    def block(i):
        start = i * Q_BLOCK
        qb = lax.dynamic_slice_in_dim(q, start, Q_BLOCK, axis=1)
        iqb = lax.dynamic_slice_in_dim(iq, start, Q_BLOCK, axis=1).astype(jnp.float32)
        iwb = lax.dynamic_slice_in_dim(iw, start, Q_BLOCK, axis=1).astype(jnp.float32)
        q_chunk = (start + jnp.arange(Q_BLOCK)) // CHUNK
        allowed = key_chunk[None, :] <= q_chunk[:, None]
        logits = jnp.einsum('bqhd,bsd->bqhs', iqb, ik32) * (IDX_DIM ** -0.5)
        score = jnp.einsum('bqh,bqhs->bqs', iwb, jax.nn.relu(logits))
        score = jnp.where(allowed[None], score, -jnp.inf)
        top_val, top_idx = lax.top_k(score, topk)
        valid = jnp.isfinite(top_val)
        kg = jax.vmap(lambda kk, idx: kk[idx])(k, top_idx)
        vg = jax.vmap(lambda vv, idx: vv[idx])(v, top_idx)
        s = jnp.einsum('bqhd,bqkhd->bqhk', qb.astype(jnp.float32), kg.astype(jnp.float32)) * (HEAD_DIM ** -0.5)
        s = jnp.where(valid[:, :, None, :], s, -jnp.inf)
        p = jax.nn.softmax(s, axis=-1)
        o = jnp.einsum('bqhk,bqkhd->bqhd', p, vg.astype(jnp.float32))
        return o.astype(q.dtype)

    out = lax.map(block, jnp.arange(n_blocks))
    return jnp.transpose(out, (1, 0, 2, 3, 4)).reshape(B, S, N_HEADS * HEAD_DIM)


def setup_inputs(seed: int = 0) -> dict:
    key = jax.random.key(seed)
    ks = jax.random.split(key, 12)
    f32 = jnp.float32
    def gain(k):
        return 1.0 + 0.02 * jax.random.normal(k, (DEPTH, D_MODEL), f32)
    x = jax.random.normal(ks[0], (BATCH, SEQ, D_MODEL), f32)
    positions = jnp.broadcast_to(jnp.arange(SEQ, dtype=jnp.int32)[None, :], (BATCH, SEQ)).astype(jnp.int32)
    return {
        "x": x,
        "positions": positions,
        "g_pre_mix": gain(ks[1]),
        "w_in": jax.random.normal(ks[2], (DEPTH, D_MODEL, IN_WIDTH), f32) * D_MODEL ** -0.5,
        "w_pool": jax.random.normal(ks[3], (DEPTH, N_POOL_GROUPS, POOL_GROUP, POOL_GROUP), f32) * POOL_GROUP ** -0.5,
        "pool_scale": 1.0 + 0.02 * jax.random.normal(ks[4], (DEPTH, POOL_WIDTH), f32),
        "w_out": jax.random.normal(ks[5], (DEPTH, D_MODEL, D_MODEL), f32) * D_MODEL ** -0.5,
        "g_post_mix": gain(ks[6]),
        "g_pre_ffn": gain(ks[7]),
        "w_ff1": jax.random.normal(ks[8], (DEPTH, D_MODEL, D_FF), f32) * D_MODEL ** -0.5,
        "w_ff2": jax.random.normal(ks[9], (DEPTH, D_FF, D_MODEL), f32) * D_FF ** -0.5,
        "g_post_ffn": gain(ks[10]),
    }


def reference(x, positions, g_pre_mix, w_in, w_pool, pool_scale, w_out, g_post_mix,
              g_pre_ffn, w_ff1, w_ff2, g_post_ffn):
    B, S, _ = x.shape
    topk = min(TOPK_MAX, S // 4)
    for l in range(DEPTH):
        h = rms_norm(x, g_pre_mix[l])
        proj = h @ w_in[l]
        u_pool, q, k, v, iq, ik, iw = jnp.split(proj, SPLIT_POINTS, axis=-1)
        q = rope_partial(q.reshape(B, S, N_HEADS, HEAD_DIM), positions, ROPE_DIM)
        k = rope_partial(k.reshape(B, S, N_HEADS, HEAD_DIM), positions, ROPE_DIM)
        v = v.reshape(B, S, N_HEADS, HEAD_DIM)
        iq = rope_partial(iq.reshape(B, S, IDX_HEADS, IDX_DIM), positions, IDX_ROPE_DIM)
        ik = rope_partial(ik[:, :, None, :], positions, IDX_ROPE_DIM)[:, :, 0, :]
        iw = iw * (IDX_HEADS ** -0.5)
        a_out = pool_mixer(u_pool, w_pool[l], pool_scale[l])
        b_out = dsa_attention(q, k, v, iq, ik, iw, topk)
        mix = jnp.concatenate([a_out, b_out], axis=-1) @ w_out[l]
        x = x + rms_norm(mix, g_post_mix[l])
        h = rms_norm(x, g_pre_ffn[l])
        f = jnp.square(jax.nn.relu(h @ w_ff1[l])) @ w_ff2[l]
        x = x + rms_norm(f, g_post_ffn[l])
    return x
```

```python
import functools

import jax
import jax.numpy as jnp
from jax import lax
from jax.experimental import pallas as pl
from jax.experimental.pallas import tpu as pltpu

F32 = jnp.float32
BF16 = jnp.bfloat16
I32 = jnp.int32

D_MODEL = 1024
CHUNK_SHIFT = 6
POOL_WIDTH = 512
POOL_WINDOWS = (2, 4, 8, 16)
POOL_GROUP = 128
POOL_HALO = 16
ATTN_WIDTH = 512
HEAD_DIM = 64
N_HEADS = 8
N_PAIRS = N_HEADS // 2
ROPE_HALF = 8
ROPE_THETA = 500000.0
IDX_HEADS = 8
IDX_DIM = 64
TOPK = 256
D_FF = 4096
EPS = 1e-6
LANES = 128
IW_ROWS = 16

ROW_TILE = 512
TQ = 256
TK = ROW_TILE
SCORE_ROWS = 64
FF_CHUNK = 1024
MASK_NEG = -1e30
SEARCH_CAP = 40

NT_DIMS = (((1,), (1,)), ((), ()))


def _rms(x, g):
    return x * lax.rsqrt(jnp.mean(x * x, axis=-1, keepdims=True) + EPS) * g


def _tree(fn, xs):
    xs = list(xs)
    while len(xs) > 1:
        nxt = [fn(xs[i], xs[i + 1]) for i in range(0, len(xs) - 1, 2)]
        if len(xs) % 2:
            nxt.append(xs[-1])
        xs = nxt
    return xs[0]


def _rows8(fn, x):
    return _tree(fn, [x[i:i + 8] for i in range(0, x.shape[0], 8)])


def _proj_kernel(x_ref, g_ref, w_ref, wvt_ref, wiwt_ref, cos_ref, sin_ref, wpool_ref, pscale_ref,
                 a_ref, q_ref, k_ref, vt_ref, iq_ref, ikk_ref, iwt_ref, ubuf_ref, *, tiles_per_seq):
    tm = x_ref.shape[0]
    t_in_seq = pl.program_id(0) % tiles_per_seq
    h = _rms(x_ref[...], g_ref[...]).astype(BF16)

    u = jnp.dot(h, w_ref[:, 0:POOL_WIDTH], preferred_element_type=F32)

    @pl.when(t_in_seq == 0)
    def _():
        ubuf_ref[0:POOL_HALO, :] = jnp.zeros((POOL_HALO, POOL_WIDTH), F32)

    @pl.when(t_in_seq != 0)
    def _():
        ubuf_ref[0:POOL_HALO, :] = ubuf_ref[tm:tm + POOL_HALO, :]

    ubuf_ref[POOL_HALO:POOL_HALO + tm, :] = u
    pos = t_in_seq * tm + lax.broadcasted_iota(I32, (tm, 1), 0)
    for g, w in enumerate(POOL_WINDOWS):
        c0 = g * POOL_GROUP
        ug = u[:, c0:c0 + POOL_GROUP]
        ssum = ug
        for j in range(1, w):
            ssum = ssum + ubuf_ref[POOL_HALO - j:POOL_HALO - j + tm, c0:c0 + POOL_GROUP]
        cnt = jnp.minimum(pos + 1, w).astype(F32)
        d = (ssum / cnt - ug).astype(BF16)
        y = jnp.dot(d, wpool_ref[g], preferred_element_type=F32) * pscale_ref[:, c0:c0 + POOL_GROUP]
        a_ref[:, c0:c0 + POOL_GROUP] = y.astype(BF16)

    cos_t = cos_ref[...]
    sin_t = sin_ref[...]
    first_half = (lax.broadcasted_iota(I32, (tm, LANES), 1) % HEAD_DIM) < ROPE_HALF

    def rope(z):
        partner = jnp.where(first_half, pltpu.roll(z, LANES - ROPE_HALF, 1), pltpu.roll(z, ROPE_HALF, 1))
        return z * cos_t + partner * sin_t

    def head_group(col0, out_ref, scale):
        z = jnp.dot(h, w_ref[:, col0:col0 + ATTN_WIDTH], preferred_element_type=F32)
        for j in range(N_PAIRS):
            r = rope(z[:, j * LANES:(j + 1) * LANES])
            if scale != 1.0:
                r = r * scale
            out_ref[j] = r.astype(BF16)

    head_group(POOL_WIDTH, q_ref, HEAD_DIM ** -0.5)
    head_group(POOL_WIDTH + ATTN_WIDTH, k_ref, 1.0)
    head_group(POOL_WIDTH + 2 * ATTN_WIDTH, iq_ref, IDX_DIM ** -0.5)
    col_ik = POOL_WIDTH + 3 * ATTN_WIDTH
    ikk = jnp.dot(h, w_ref[:, col_ik:col_ik + LANES], preferred_element_type=F32)
    ikk_ref[...] = rope(ikk).astype(BF16)

    vt_ref[0] = lax.dot_general(wvt_ref[...], h, NT_DIMS, preferred_element_type=F32).astype(BF16)
    iwt_ref[...] = lax.dot_general(wiwt_ref[...], h, NT_DIMS, preferred_element_type=F32) * (IDX_HEADS ** -0.5)


def _const_spec(shape):
    return pl.BlockSpec(shape, lambda *_: (0,) * len(shape))


def _proj_call(x2, g, w_main, wvt, wiwt, cos_t, sin_t, wpool, pscale, *, seq):
    n = x2.shape[0]
    tm = ROW_TILE
    tps = seq // tm
    row = lambda width: pl.BlockSpec((tm, width), lambda i: (i, 0))
    pair = pl.BlockSpec((N_PAIRS, tm, LANES), lambda i: (0, i, 0))
    pair_shape = jax.ShapeDtypeStruct((N_PAIRS, n, LANES), BF16)
    return pl.pallas_call(
        functools.partial(_proj_kernel, tiles_per_seq=tps),
        grid=(n // tm,),
        in_specs=[row(D_MODEL), _const_spec((1, D_MODEL)), _const_spec(w_main.shape), _const_spec(wvt.shape),
                  _const_spec(wiwt.shape), row(LANES), row(LANES), _const_spec(wpool.shape),
                  _const_spec((1, POOL_WIDTH))],
        out_specs=[row(POOL_WIDTH), pair, pair,
                   pl.BlockSpec((1, ATTN_WIDTH, tm), lambda i: (i, 0, 0)),
                   pair, row(LANES), pl.BlockSpec((IW_ROWS, tm), lambda i: (0, i))],
        out_shape=[jax.ShapeDtypeStruct((n, POOL_WIDTH), BF16), pair_shape, pair_shape,
                   jax.ShapeDtypeStruct((n // tm, ATTN_WIDTH, tm), BF16),
                   pair_shape, jax.ShapeDtypeStruct((n, LANES), BF16),
                   jax.ShapeDtypeStruct((IW_ROWS, n), F32)],
        scratch_shapes=[pltpu.VMEM((POOL_HALO + tm, POOL_WIDTH), F32)],
        compiler_params=pltpu.CompilerParams(dimension_semantics=("arbitrary",)),
        name="proj",
    )(x2, g, w_main, wvt, wiwt, cos_t, sin_t, wpool, pscale)


def _key_of(x):
    b = lax.bitcast_convert_type(x, I32)
    return b ^ ((b >> 31) & 0x7FFFFFFF)


def _val_of(key):
    return lax.bitcast_convert_type(key ^ ((key >> 31) & 0x7FFFFFFF), F32)


def _attn_kernel(q_ref, k_ref, vt_ref, iq_ref, ikk_ref, iwt_ref, o_ref,
                 qs_ref, iqs_ref, st_ref, lg_ref, rmax_ref, rmin_ref, m_ref, ls_ref, acc_ref, ot_ref):
    tq = q_ref.shape[1]
    tk = vt_ref.shape[2]
    q0 = pl.program_id(1) * tq
    n_kt = (q0 + tq + tk - 1) // tk
    q_chunk = (q0 + lax.broadcasted_iota(I32, (1, tq), 1)) >> CHUNK_SHIFT

    low = lax.broadcasted_iota(I32, (tq, LANES), 1) < HEAD_DIM
    for j in range(N_PAIRS):
        qp = q_ref[j].astype(F32)
        qs_ref[2 * j] = jnp.where(low, qp, 0.0).astype(BF16)
        qs_ref[2 * j + 1] = jnp.where(low, 0.0, qp).astype(BF16)
        ip = iq_ref[j].astype(F32)
        iqs_ref[2 * j * tq:(2 * j + 1) * tq, :] = jnp.where(low, ip, 0.0).astype(BF16)
        iqs_ref[(2 * j + 1) * tq:(2 * j + 2) * tq, :] = jnp.where(low, 0.0, ip).astype(BF16)

    def score_tile(kt, masked):
        r0 = pl.multiple_of(kt * tk, tk)
        lg_ref[...] = lax.dot_general(ikk_ref[pl.ds(r0, tk), :], iqs_ref[...], NT_DIMS,
                                      preferred_element_type=F32)
        mx = rmax_ref[...]
        mn = rmin_ref[...]
        for r in range(tk // SCORE_ROWS):
            rows = slice(r * SCORE_ROWS, (r + 1) * SCORE_ROWS)
            sc = None
            for hh in range(IDX_HEADS):
                c = iwt_ref[hh:hh + 1, :] * jnp.maximum(lg_ref[rows, hh * tq:(hh + 1) * tq], 0.0)
                sc = c if sc is None else sc + c
            if masked:
                k_idx = r0 + r * SCORE_ROWS + lax.broadcasted_iota(I32, (SCORE_ROWS, tq), 0)
                allowed = (k_idx >> CHUNK_SHIFT) <= q_chunk
                sc_hi = jnp.where(allowed, sc, -jnp.inf)
                sc_lo = jnp.where(allowed, sc, jnp.inf)
            else:
                sc_hi = sc_lo = sc
            st_ref[pl.ds(r0 + r * SCORE_ROWS, SCORE_ROWS), :] = sc_hi
            mx = jnp.maximum(mx, _rows8(jnp.maximum, sc_hi))
            mn = jnp.minimum(mn, _rows8(jnp.minimum, sc_lo))
        rmax_ref[...] = mx
        rmin_ref[...] = mn

    rmax_ref[...] = jnp.full(rmax_ref.shape, -jnp.inf, F32)
    rmin_ref[...] = jnp.full(rmin_ref.shape, jnp.inf, F32)

    def score_body(kt, c):
        score_tile(kt, False)
        return c

    lax.fori_loop(0, n_kt - 1, score_body, 0)
    score_tile(n_kt - 1, True)

    def count(pred):
        def body(kt, acc):
            blk = st_ref[pl.ds(pl.multiple_of(kt * tk, tk), tk), :]
            return acc + _rows8(jnp.add, jnp.where(pred(blk), 1.0, 0.0))
        acc = lax.fori_loop(0, n_kt, body, jnp.zeros((8, tq), F32))
        return jnp.sum(acc, axis=0, keepdims=True)

    n_allowed = ((q_chunk + 1) << CHUNK_SHIFT).astype(F32)
    lo0 = _key_of(jnp.min(rmin_ref[...], axis=0, keepdims=True))
    hi0 = _key_of(jnp.max(rmax_ref[...], axis=0, keepdims=True)) + 1
    hi0 = jnp.where(n_allowed > TOPK, hi0, lo0 + 1)

    def any_open(lo, hi):
        return jnp.max(jnp.where(hi != lo + 1, 1.0, 0.0))

    def search_cond(state):
        return jnp.logical_and(state[0] < SEARCH_CAP, state[1] > 0.5)

    def search_body(state):
        it, _, lo, hi, c_lo = state
        is_open = hi != lo + 1
        mid = (lo >> 1) + (hi >> 1) + (lo & hi & 1)
        t = _val_of(mid)
        c = count(lambda blk: blk >= t)
        ge = c >= TOPK
        lo_n = jnp.where(is_open & ge, mid, lo)
        c_lo_n = jnp.where(is_open & ge, c, c_lo)
        hi_n = jnp.where(is_open & jnp.logical_not(ge), mid, hi)
        hi_n = jnp.where(is_open & (c == TOPK), mid + 1, hi_n)
        return it + 1, any_open(lo_n, hi_n), lo_n, hi_n, c_lo_n

    _, _, lo, _, c_lo = lax.while_loop(search_cond, search_body,
                                       (jnp.int32(0), any_open(lo0, hi0), lo0, hi0, n_allowed))
    thr = _val_of(lo)

    has_tie = jnp.max(jnp.where(c_lo > TOPK, 1.0, 0.0)) > 0.5

    @pl.when(jnp.logical_not(has_tie))
    def _():
        def body(kt, c):
            rows = pl.ds(pl.multiple_of(kt * tk, tk), tk)
            st_ref[rows, :] = jnp.where(st_ref[rows, :] >= thr, 0.0, MASK_NEG)
            return c
        lax.fori_loop(0, n_kt, body, 0)

    @pl.when(has_tie)
    def _():
        budget = TOPK - count(lambda blk: blk > thr)
        ri = lax.broadcasted_iota(I32, (tk, tk), 0)
        ci = lax.broadcasted_iota(I32, (tk, tk), 1)
        prefix = jnp.where(ci <= ri, 1.0, 0.0).astype(BF16)

        def body(kt, seen):
            rows = pl.ds(pl.multiple_of(kt * tk, tk), tk)
            blk = st_ref[rows, :]
            eq = blk == thr
            rank = jnp.dot(prefix, jnp.where(eq, 1.0, 0.0).astype(BF16), preferred_element_type=F32) + seen
            keep = (blk > thr) | (eq & (rank <= budget))
            st_ref[rows, :] = jnp.where(keep, 0.0, MASK_NEG)
            return rank[tk - 1:tk, :]
        lax.fori_loop(0, n_kt, body, jnp.zeros((1, tq), F32))

    def head_body(hh, c):
        qh = qs_ref[hh]
        pair = hh // 2
        m_ref[...] = jnp.full(m_ref.shape, -jnp.inf, F32)
        ls_ref[...] = jnp.zeros(ls_ref.shape, F32)
        acc_ref[...] = jnp.zeros(acc_ref.shape, F32)

        def kt_body(kt, c2):
            r0 = pl.multiple_of(kt * tk, tk)
            s = lax.dot_general(k_ref[pair, pl.ds(r0, tk), :], qh, NT_DIMS, preferred_element_type=F32)
            s = s + st_ref[pl.ds(r0, tk), :]
            m_prev = m_ref[...]
            m_new = jnp.maximum(m_prev, jnp.max(s, axis=0, keepdims=True))
            alpha = jnp.exp(m_prev - m_new)
            p = jnp.exp(s - m_new)
            ls_ref[...] = alpha * ls_ref[...] + jnp.sum(p, axis=0, keepdims=True)
            v_t = vt_ref[kt, pl.ds(pl.multiple_of(hh * HEAD_DIM, HEAD_DIM), HEAD_DIM), :]
            acc_ref[...] = alpha * acc_ref[...] + jnp.dot(v_t, p.astype(BF16), preferred_element_type=F32)
            m_ref[...] = m_new
            return c2

        lax.fori_loop(0, n_kt, kt_body, 0)
        ot_ref[pl.ds(pl.multiple_of(hh * HEAD_DIM, HEAD_DIM), HEAD_DIM), :] = acc_ref[...] / ls_ref[...]
        return c

    lax.fori_loop(0, N_HEADS, head_body, 0)
    o_ref[...] = ot_ref[...].T.astype(BF16)


def _attn_call(q, k, vt, iq, ikk, iwt, *, batch, seq):
    n = batch * seq
    nq = seq // TQ
    n_vt = seq // TK
    qspec = pl.BlockSpec((N_PAIRS, TQ, LANES), lambda b, i: (0, b * nq + i, 0))
    return pl.pallas_call(
        _attn_kernel,
        grid=(batch, nq),
        in_specs=[qspec,
                  pl.BlockSpec((N_PAIRS, seq, LANES), lambda b, i: (0, b, 0)),
                  pl.BlockSpec((n_vt, ATTN_WIDTH, TK), lambda b, i: (b, 0, 0)),
                  qspec,
                  pl.BlockSpec((seq, LANES), lambda b, i: (b, 0)),
                  pl.BlockSpec((IW_ROWS, TQ), lambda b, i: (0, b * nq + i))],
        out_specs=pl.BlockSpec((TQ, ATTN_WIDTH), lambda b, i: (b * nq + i, 0)),
        out_shape=jax.ShapeDtypeStruct((n, ATTN_WIDTH), BF16),
        scratch_shapes=[pltpu.VMEM((N_HEADS, TQ, LANES), BF16),
                        pltpu.VMEM((IDX_HEADS * TQ, LANES), BF16),
                        pltpu.VMEM((seq, TQ), F32),
                        pltpu.VMEM((TK, IDX_HEADS * TQ), F32),
                        pltpu.VMEM((8, TQ), F32), pltpu.VMEM((8, TQ), F32),
                        pltpu.VMEM((1, TQ), F32), pltpu.VMEM((1, TQ), F32),
                        pltpu.VMEM((HEAD_DIM, TQ), F32),
                        pltpu.VMEM((ATTN_WIDTH, TQ), F32)],
        compiler_params=pltpu.CompilerParams(dimension_semantics=("arbitrary", "arbitrary")),
        name="attn",
    )(q, k, vt, iq, ikk, iwt)


def _post_kernel(x_ref, a_ref, b_ref, woa_ref, wob_ref, gmix_ref, gffn_ref, w1_ref, w2_ref, gout_ref, o_ref):
    mix = (jnp.dot(a_ref[...], woa_ref[...], preferred_element_type=F32)
           + jnp.dot(b_ref[...], wob_ref[...], preferred_element_type=F32))
    x1 = x_ref[...] + _rms(mix, gmix_ref[...])
    h = _rms(x1, gffn_ref[...]).astype(BF16)
    f = None
    for c in range(D_FF // FF_CHUNK):
        cols = slice(c * FF_CHUNK, (c + 1) * FF_CHUNK)
        f1 = jnp.square(jnp.maximum(jnp.dot(h, w1_ref[:, cols], preferred_element_type=F32), 0.0)).astype(BF16)
        part = jnp.dot(f1, w2_ref[cols, :], preferred_element_type=F32)
        f = part if f is None else f + part
    o_ref[...] = x1 + _rms(f, gout_ref[...])


def _post_call(x2, a, b, woa, wob, gmix, gffn, w1, w2, gout):
    n = x2.shape[0]
    tm = ROW_TILE
    row = lambda width: pl.BlockSpec((tm, width), lambda i: (i, 0))
    gspec = _const_spec((1, D_MODEL))
    return pl.pallas_call(
        _post_kernel,
        grid=(n // tm,),
        in_specs=[row(D_MODEL), row(POOL_WIDTH), row(ATTN_WIDTH), _const_spec(woa.shape), _const_spec(wob.shape),
                  gspec, gspec, _const_spec(w1.shape), _const_spec(w2.shape), gspec],
        out_specs=row(D_MODEL),
        out_shape=jax.ShapeDtypeStruct((n, D_MODEL), F32),
        compiler_params=pltpu.CompilerParams(dimension_semantics=("arbitrary",),
                                             vmem_limit_bytes=56 * 1024 * 1024),
        name="post",
    )(x2, a, b, woa, wob, gmix, gffn, w1, w2, gout)


def _rope_tables(positions):
    inv = ROPE_THETA ** (-jnp.arange(ROPE_HALF, dtype=F32) / ROPE_HALF)
    ang = positions.astype(F32).reshape(-1, 1) * inv
    cos, sin = jnp.cos(ang), jnp.sin(ang)
    rest = HEAD_DIM - 2 * ROPE_HALF
    cos64 = jnp.concatenate([cos, cos, jnp.ones((cos.shape[0], rest), F32)], axis=-1)
    sin64 = jnp.concatenate([-sin, sin, jnp.zeros((sin.shape[0], rest), F32)], axis=-1)
    return jnp.tile(cos64, (1, LANES // HEAD_DIM)), jnp.tile(sin64, (1, LANES // HEAD_DIM))


def kernel(x, positions, g_pre_mix, w_in, w_pool, pool_scale, w_out, g_post_mix, g_pre_ffn, w_ff1, w_ff2,
           g_post_ffn):
    batch, seq, _ = x.shape
    depth = w_in.shape[0]
    assert seq % ROW_TILE == 0 and seq % TQ == 0 and TK % TQ == 0 and seq // 4 >= TOPK
    cos_t, sin_t = _rope_tables(positions)
    x2 = x.reshape(batch * seq, D_MODEL)
    c_q = POOL_WIDTH
    c_v = POOL_WIDTH + 2 * ATTN_WIDTH
    c_iq = c_v + ATTN_WIDTH
    c_ik = c_iq + IDX_HEADS * IDX_DIM
    c_iw = c_ik + IDX_DIM
    for l in range(depth):
        wl = w_in[l]
        w_ik = wl[:, c_ik:c_iw]
        w_main = jnp.concatenate([wl[:, :c_v], wl[:, c_iq:c_ik], w_ik, w_ik], axis=1).astype(BF16)
        wvt = wl[:, c_v:c_iq].T.astype(BF16)
        wiwt = jnp.pad(wl[:, c_iw:].T, ((0, IW_ROWS - IDX_HEADS), (0, 0))).astype(BF16)
        a, q, k, vt, iq, ikk, iwt = _proj_call(
            x2, g_pre_mix[l][None], w_main, wvt, wiwt, cos_t, sin_t, w_pool[l].astype(BF16),
            pool_scale[l][None], seq=seq)
        b = _attn_call(q, k, vt, iq, ikk, iwt, batch=batch, seq=seq)
        wo = w_out[l].astype(BF16)
        x2 = _post_call(x2, a, b, wo[:POOL_WIDTH], wo[POOL_WIDTH:], g_post_mix[l][None], g_pre_ffn[l][None],
                        w_ff1[l].astype(BF16), w_ff2[l].astype(BF16), g_post_ffn[l][None])
    del c_q
    return x2.reshape(batch, seq, D_MODEL)
```

```python
import functools

import jax
import jax.numpy as jnp
from jax import lax
from jax.experimental import pallas as pl
from jax.experimental.pallas import tpu as pltpu

F32 = jnp.float32
BF16 = jnp.bfloat16
I32 = jnp.int32

D_MODEL = 1024
CHUNK_SHIFT = 6
POOL_WIDTH = 512
POOL_WINDOWS = (2, 4, 8, 16)
POOL_GROUP = 128
POOL_HALO = 16
ATTN_WIDTH = 512
HEAD_DIM = 64
N_HEADS = 8
N_PAIRS = N_HEADS // 2
ROPE_HALF = 8
ROPE_THETA = 500000.0
IDX_HEADS = 8
IDX_DIM = 64
TOPK = 256
D_FF = 4096
EPS = 1e-6
LANES = 128
IW_ROWS = 16

ROW_TILE = 512
TQ = 256
TK = ROW_TILE
SCORE_ROWS = 64
FF_CHUNK = 1024
MASK_NEG = -1e30
LOG2_E = 1.4426950408889634
SEARCH_CAP = 40

NT_DIMS = (((1,), (1,)), ((), ()))


def _rms(x, g):
    return x * lax.rsqrt(jnp.mean(x * x, axis=-1, keepdims=True) + EPS) * g


def _tree(fn, xs):
    xs = list(xs)
    while len(xs) > 1:
        nxt = [fn(xs[i], xs[i + 1]) for i in range(0, len(xs) - 1, 2)]
        if len(xs) % 2:
            nxt.append(xs[-1])
        xs = nxt
    return xs[0]


def _rows8(fn, x):
    return _tree(fn, [x[i:i + 8] for i in range(0, x.shape[0], 8)])


def _proj_kernel(x_ref, g_ref, w_ref, wvt_ref, wiwt_ref, cos_ref, sin_ref, wpool_ref, pscale_ref,
                 a_ref, q_ref, k_ref, vt_ref, iq_ref, ikk_ref, iwt_ref, ubuf_ref, *, tiles_per_seq):
    tm = x_ref.shape[0]
    t_in_seq = pl.program_id(0) % tiles_per_seq
    h = _rms(x_ref[...], g_ref[...]).astype(BF16)

    u = jnp.dot(h, w_ref[:, 0:POOL_WIDTH], preferred_element_type=F32)

    @pl.when(t_in_seq == 0)
    def _():
        ubuf_ref[0:POOL_HALO, :] = jnp.zeros((POOL_HALO, POOL_WIDTH), F32)

    @pl.when(t_in_seq != 0)
    def _():
        ubuf_ref[0:POOL_HALO, :] = ubuf_ref[tm:tm + POOL_HALO, :]

    ubuf_ref[POOL_HALO:POOL_HALO + tm, :] = u
    pos = t_in_seq * tm + lax.broadcasted_iota(I32, (tm, 1), 0)
    for g, w in enumerate(POOL_WINDOWS):
        c0 = g * POOL_GROUP
        ug = u[:, c0:c0 + POOL_GROUP]
        ssum = ug
        for j in range(1, w):
            ssum = ssum + ubuf_ref[POOL_HALO - j:POOL_HALO - j + tm, c0:c0 + POOL_GROUP]
        cnt = jnp.minimum(pos + 1, w).astype(F32)
        d = (ssum / cnt - ug).astype(BF16)
        y = jnp.dot(d, wpool_ref[g], preferred_element_type=F32) * pscale_ref[:, c0:c0 + POOL_GROUP]
        a_ref[:, c0:c0 + POOL_GROUP] = y.astype(BF16)

    cos_t = cos_ref[...]
    sin_t = sin_ref[...]
    first_half = (lax.broadcasted_iota(I32, (tm, LANES), 1) % HEAD_DIM) < ROPE_HALF

    def rope(z):
        partner = jnp.where(first_half, pltpu.roll(z, LANES - ROPE_HALF, 1), pltpu.roll(z, ROPE_HALF, 1))
        return z * cos_t + partner * sin_t

    def head_group(col0, out_ref, scale):
        z = jnp.dot(h, w_ref[:, col0:col0 + ATTN_WIDTH], preferred_element_type=F32)
        for j in range(N_PAIRS):
            r = rope(z[:, j * LANES:(j + 1) * LANES])
            if scale != 1.0:
                r = r * scale
            out_ref[j] = r.astype(BF16)

    head_group(POOL_WIDTH, q_ref, HEAD_DIM ** -0.5 * LOG2_E)
    head_group(POOL_WIDTH + ATTN_WIDTH, k_ref, 1.0)
    head_group(POOL_WIDTH + 2 * ATTN_WIDTH, iq_ref, IDX_DIM ** -0.5)
    col_ik = POOL_WIDTH + 3 * ATTN_WIDTH
    ikk = jnp.dot(h, w_ref[:, col_ik:col_ik + LANES], preferred_element_type=F32)
    ikk_ref[...] = rope(ikk).astype(BF16)

    vt_ref[0] = lax.dot_general(wvt_ref[...], h, NT_DIMS, preferred_element_type=F32).astype(BF16)
    iwt_ref[...] = lax.dot_general(wiwt_ref[...], h, NT_DIMS, preferred_element_type=F32) * (IDX_HEADS ** -0.5)


def _const_spec(shape):
    return pl.BlockSpec(shape, lambda *_: (0,) * len(shape))


def _proj_call(x2, g, w_main, wvt, wiwt, cos_t, sin_t, wpool, pscale, *, seq):
    n = x2.shape[0]
    tm = ROW_TILE
    tps = seq // tm
    row = lambda width: pl.BlockSpec((tm, width), lambda i: (i, 0))
    pair = pl.BlockSpec((N_PAIRS, tm, LANES), lambda i: (0, i, 0))
    pair_shape = jax.ShapeDtypeStruct((N_PAIRS, n, LANES), BF16)
    return pl.pallas_call(
        functools.partial(_proj_kernel, tiles_per_seq=tps),
        grid=(n // tm,),
        in_specs=[row(D_MODEL), _const_spec((1, D_MODEL)), _const_spec(w_main.shape), _const_spec(wvt.shape),
                  _const_spec(wiwt.shape), row(LANES), row(LANES), _const_spec(wpool.shape),
                  _const_spec((1, POOL_WIDTH))],
        out_specs=[row(POOL_WIDTH), pair, pair,
                   pl.BlockSpec((1, ATTN_WIDTH, tm), lambda i: (i, 0, 0)),
                   pair, row(LANES), pl.BlockSpec((IW_ROWS, tm), lambda i: (0, i))],
        out_shape=[jax.ShapeDtypeStruct((n, POOL_WIDTH), BF16), pair_shape, pair_shape,
                   jax.ShapeDtypeStruct((n // tm, ATTN_WIDTH, tm), BF16),
                   pair_shape, jax.ShapeDtypeStruct((n, LANES), BF16),
                   jax.ShapeDtypeStruct((IW_ROWS, n), F32)],
        scratch_shapes=[pltpu.VMEM((POOL_HALO + tm, POOL_WIDTH), F32)],
        compiler_params=pltpu.CompilerParams(dimension_semantics=("arbitrary",)),
        name="proj",
    )(x2, g, w_main, wvt, wiwt, cos_t, sin_t, wpool, pscale)


def _key_of(x):
    b = lax.bitcast_convert_type(x, I32)
    return b ^ ((b >> 31) & 0x7FFFFFFF)


def _val_of(key):
    return lax.bitcast_convert_type(key ^ ((key >> 31) & 0x7FFFFFFF), F32)


def _attn_kernel(q_ref, k_ref, vt_ref, iq_ref, ikk_ref, iwt_ref, o_ref,
                 qs_ref, iqs_ref, st_ref, lg_ref, rmax_ref, rmin_ref, m_ref, ls_ref, ot_ref,
                 s_ref, p_ref):
    tq = q_ref.shape[1]
    tk = vt_ref.shape[2]
    q0 = pl.program_id(1) * tq
    n_kt = (q0 + tq + tk - 1) // tk
    q_chunk = (q0 + lax.broadcasted_iota(I32, (1, tq), 1)) >> CHUNK_SHIFT

    low = lax.broadcasted_iota(I32, (tq, LANES), 1) < HEAD_DIM
    for j in range(N_PAIRS):
        qp = q_ref[j].astype(F32)
        qs_ref[2 * j] = jnp.where(low, qp, 0.0).astype(BF16)
        qs_ref[2 * j + 1] = jnp.where(low, 0.0, qp).astype(BF16)
        ip = iq_ref[j].astype(F32)
        iqs_ref[2 * j * tq:(2 * j + 1) * tq, :] = jnp.where(low, ip, 0.0).astype(BF16)
        iqs_ref[(2 * j + 1) * tq:(2 * j + 2) * tq, :] = jnp.where(low, 0.0, ip).astype(BF16)

    def score_tile(kt, masked):
        r0 = pl.multiple_of(kt * tk, tk)
        lg_ref[...] = lax.dot_general(ikk_ref[pl.ds(r0, tk), :], iqs_ref[...], NT_DIMS,
                                      preferred_element_type=F32)
        mx = rmax_ref[...]
        mn = rmin_ref[...]
        for r in range(tk // SCORE_ROWS):
            rows = slice(r * SCORE_ROWS, (r + 1) * SCORE_ROWS)
            sc = None
            for hh in range(IDX_HEADS):
                c = iwt_ref[hh:hh + 1, :] * jnp.maximum(lg_ref[rows, hh * tq:(hh + 1) * tq], 0.0)
                sc = c if sc is None else sc + c
            if masked:
                k_idx = r0 + r * SCORE_ROWS + lax.broadcasted_iota(I32, (SCORE_ROWS, tq), 0)
                allowed = (k_idx >> CHUNK_SHIFT) <= q_chunk
                sc_hi = jnp.where(allowed, sc, -jnp.inf)
                sc_lo = jnp.where(allowed, sc, jnp.inf)
            else:
                sc_hi = sc_lo = sc
            st_ref[pl.ds(r0 + r * SCORE_ROWS, SCORE_ROWS), :] = sc_hi
            mx = jnp.maximum(mx, _rows8(jnp.maximum, sc_hi))
            mn = jnp.minimum(mn, _rows8(jnp.minimum, sc_lo))
        rmax_ref[...] = mx
        rmin_ref[...] = mn

    rmax_ref[...] = jnp.full(rmax_ref.shape, -jnp.inf, F32)
    rmin_ref[...] = jnp.full(rmin_ref.shape, jnp.inf, F32)

    def score_body(kt, c):
        score_tile(kt, False)
        return c

    lax.fori_loop(0, n_kt - 1, score_body, 0)
    score_tile(n_kt - 1, True)

    def count(pred):
        def body(kt, acc):
            blk = st_ref[pl.ds(pl.multiple_of(kt * tk, tk), tk), :]
            return acc + _rows8(jnp.add, jnp.where(pred(blk), 1.0, 0.0))
        acc = lax.fori_loop(0, n_kt, body, jnp.zeros((8, tq), F32))
        return jnp.sum(acc, axis=0, keepdims=True)

    n_allowed = ((q_chunk + 1) << CHUNK_SHIFT).astype(F32)
    lo0 = _key_of(jnp.min(rmin_ref[...], axis=0, keepdims=True))
    hi0 = _key_of(jnp.max(rmax_ref[...], axis=0, keepdims=True)) + 1
    hi0 = jnp.where(n_allowed > TOPK, hi0, lo0 + 1)

    def any_open(lo, hi):
        return jnp.max(jnp.where(hi != lo + 1, 1.0, 0.0))

    def search_cond(state):
        return jnp.logical_and(state[0] < SEARCH_CAP, state[1] > 0.5)

    def search_body(state):
        it, _, lo, hi, c_lo = state
        is_open = hi != lo + 1
        mid = (lo >> 1) + (hi >> 1) + (lo & hi & 1)
        t = _val_of(mid)
        c = count(lambda blk: blk >= t)
        ge = c >= TOPK
        lo_n = jnp.where(is_open & ge, mid, lo)
        c_lo_n = jnp.where(is_open & ge, c, c_lo)
        hi_n = jnp.where(is_open & jnp.logical_not(ge), mid, hi)
        hi_n = jnp.where(is_open & (c == TOPK), mid + 1, hi_n)
        return it + 1, any_open(lo_n, hi_n), lo_n, hi_n, c_lo_n

    _, _, lo, _, c_lo = lax.while_loop(search_cond, search_body,
                                       (jnp.int32(0), any_open(lo0, hi0), lo0, hi0, n_allowed))
    thr = _val_of(lo)

    has_tie = jnp.max(jnp.where(c_lo > TOPK, 1.0, 0.0)) > 0.5

    @pl.when(jnp.logical_not(has_tie))
    def _():
        def body(kt, c):
            rows = pl.ds(pl.multiple_of(kt * tk, tk), tk)
            st_ref[rows, :] = jnp.where(st_ref[rows, :] >= thr, 0.0, MASK_NEG)
            return c
        lax.fori_loop(0, n_kt, body, 0)

    @pl.when(has_tie)
    def _():
        budget = TOPK - count(lambda blk: blk > thr)
        ri = lax.broadcasted_iota(I32, (tk, tk), 0)
        ci = lax.broadcasted_iota(I32, (tk, tk), 1)
        prefix = jnp.where(ci <= ri, 1.0, 0.0).astype(BF16)

        def body(kt, seen):
            rows = pl.ds(pl.multiple_of(kt * tk, tk), tk)
            blk = st_ref[rows, :]
            eq = blk == thr
            rank = jnp.dot(prefix, jnp.where(eq, 1.0, 0.0).astype(BF16), preferred_element_type=F32) + seen
            keep = (blk > thr) | (eq & (rank <= budget))
            st_ref[rows, :] = jnp.where(keep, 0.0, MASK_NEG)
            return rank[tk - 1:tk, :]
        lax.fori_loop(0, n_kt, body, jnp.zeros((1, tq), F32))

    m_ref[...] = jnp.full(m_ref.shape, -jnp.inf, F32)
    ls_ref[...] = jnp.zeros(ls_ref.shape, F32)
    ot_ref[...] = jnp.zeros(ot_ref.shape, F32)
    n_sub = tk // SCORE_ROWS

    def stage_logits(kt, hh):
        r0 = pl.multiple_of(kt * tk, tk)
        s_ref[hh] = lax.dot_general(k_ref[hh // 2, pl.ds(r0, tk), :], qs_ref[hh], NT_DIMS,
                                    preferred_element_type=F32)

    for hh in range(N_HEADS):
        stage_logits(0, hh)

    def kt_body(kt, c):
        r0 = pl.multiple_of(kt * tk, tk)
        kt_next = jnp.minimum(kt + 1, n_kt - 1)

        def masked(hh, r):
            rows = slice(r * SCORE_ROWS, (r + 1) * SCORE_ROWS)
            return s_ref[hh, rows, :] + st_ref[pl.ds(r0 + r * SCORE_ROWS, SCORE_ROWS), :]

        for hh in range(N_HEADS):
            feat = slice(hh * HEAD_DIM, (hh + 1) * HEAD_DIM)
            m_prev = m_ref[hh]
            mx = _tree(jnp.maximum, [_rows8(jnp.maximum, masked(hh, r)) for r in range(n_sub)])
            m_new = jnp.maximum(m_prev, jnp.max(mx, axis=0, keepdims=True))
            m_ref[hh] = m_new
            alpha = jnp.exp2(m_prev - m_new)
            sums = []
            for r in range(n_sub):
                p = jnp.exp2(masked(hh, r) - m_new)
                sums.append(_rows8(jnp.add, p))
                p_ref[hh, r * SCORE_ROWS:(r + 1) * SCORE_ROWS, :] = p.astype(BF16)
            ls_ref[hh] = alpha * ls_ref[hh] + jnp.sum(_tree(jnp.add, sums), axis=0, keepdims=True)
            pv = jnp.dot(vt_ref[kt, feat, :], p_ref[hh], preferred_element_type=F32)
            ot_ref[feat, :] = alpha * ot_ref[feat, :] + pv
            stage_logits(kt_next, hh)
        return c

    lax.fori_loop(0, n_kt, kt_body, 0)
    for hh in range(N_HEADS):
        feat = slice(hh * HEAD_DIM, (hh + 1) * HEAD_DIM)
        ot_ref[feat, :] = ot_ref[feat, :] / ls_ref[hh]
    o_ref[...] = ot_ref[...].T.astype(BF16)


def _attn_call(q, k, vt, iq, ikk, iwt, *, batch, seq):
    n = batch * seq
    nq = seq // TQ
    n_vt = seq // TK
    qspec = pl.BlockSpec((N_PAIRS, TQ, LANES), lambda b, i: (0, b * nq + i, 0))
    return pl.pallas_call(
        _attn_kernel,
        grid=(batch, nq),
        in_specs=[qspec,
                  pl.BlockSpec((N_PAIRS, seq, LANES), lambda b, i: (0, b, 0)),
                  pl.BlockSpec((n_vt, ATTN_WIDTH, TK), lambda b, i: (b, 0, 0)),
                  qspec,
                  pl.BlockSpec((seq, LANES), lambda b, i: (b, 0)),
                  pl.BlockSpec((IW_ROWS, TQ), lambda b, i: (0, b * nq + i))],
        out_specs=pl.BlockSpec((TQ, ATTN_WIDTH), lambda b, i: (b * nq + i, 0)),
        out_shape=jax.ShapeDtypeStruct((n, ATTN_WIDTH), BF16),
        scratch_shapes=[pltpu.VMEM((N_HEADS, TQ, LANES), BF16),
                        pltpu.VMEM((IDX_HEADS * TQ, LANES), BF16),
                        pltpu.VMEM((seq, TQ), F32),
                        pltpu.VMEM((TK, IDX_HEADS * TQ), F32),
                        pltpu.VMEM((8, TQ), F32), pltpu.VMEM((8, TQ), F32),
                        pltpu.VMEM((N_HEADS, 1, TQ), F32), pltpu.VMEM((N_HEADS, 1, TQ), F32),
                        pltpu.VMEM((ATTN_WIDTH, TQ), F32),
                        pltpu.VMEM((N_HEADS, TK, TQ), F32),
                        pltpu.VMEM((N_HEADS, TK, TQ), BF16)],
        compiler_params=pltpu.CompilerParams(dimension_semantics=("arbitrary", "arbitrary")),
        name="attn",
    )(q, k, vt, iq, ikk, iwt)


def _post_kernel(x_ref, a_ref, b_ref, woa_ref, wob_ref, gmix_ref, gffn_ref, w1_ref, w2_ref, gout_ref, o_ref):
    mix = (jnp.dot(a_ref[...], woa_ref[...], preferred_element_type=F32)
           + jnp.dot(b_ref[...], wob_ref[...], preferred_element_type=F32))
    x1 = x_ref[...] + _rms(mix, gmix_ref[...])
    h = _rms(x1, gffn_ref[...]).astype(BF16)
    f = None
    for c in range(D_FF // FF_CHUNK):
        cols = slice(c * FF_CHUNK, (c + 1) * FF_CHUNK)
        f1 = jnp.square(jnp.maximum(jnp.dot(h, w1_ref[:, cols], preferred_element_type=F32), 0.0)).astype(BF16)
        part = jnp.dot(f1, w2_ref[cols, :], preferred_element_type=F32)
        f = part if f is None else f + part
    o_ref[...] = x1 + _rms(f, gout_ref[...])


def _post_call(x2, a, b, woa, wob, gmix, gffn, w1, w2, gout):
    n = x2.shape[0]
    tm = ROW_TILE
    row = lambda width: pl.BlockSpec((tm, width), lambda i: (i, 0))
    gspec = _const_spec((1, D_MODEL))
    return pl.pallas_call(
        _post_kernel,
        grid=(n // tm,),
        in_specs=[row(D_MODEL), row(POOL_WIDTH), row(ATTN_WIDTH), _const_spec(woa.shape), _const_spec(wob.shape),
                  gspec, gspec, _const_spec(w1.shape), _const_spec(w2.shape), gspec],
        out_specs=row(D_MODEL),
        out_shape=jax.ShapeDtypeStruct((n, D_MODEL), F32),
        compiler_params=pltpu.CompilerParams(dimension_semantics=("arbitrary",),
                                             vmem_limit_bytes=56 * 1024 * 1024),
        name="post",
    )(x2, a, b, woa, wob, gmix, gffn, w1, w2, gout)


def _rope_tables(positions):
    inv = ROPE_THETA ** (-jnp.arange(ROPE_HALF, dtype=F32) / ROPE_HALF)
    ang = positions.astype(F32).reshape(-1, 1) * inv
    cos, sin = jnp.cos(ang), jnp.sin(ang)
    rest = HEAD_DIM - 2 * ROPE_HALF
    cos64 = jnp.concatenate([cos, cos, jnp.ones((cos.shape[0], rest), F32)], axis=-1)
    sin64 = jnp.concatenate([-sin, sin, jnp.zeros((sin.shape[0], rest), F32)], axis=-1)
    return jnp.tile(cos64, (1, LANES // HEAD_DIM)), jnp.tile(sin64, (1, LANES // HEAD_DIM))


def kernel(x, positions, g_pre_mix, w_in, w_pool, pool_scale, w_out, g_post_mix, g_pre_ffn, w_ff1, w_ff2,
           g_post_ffn):
    batch, seq, _ = x.shape
    depth = w_in.shape[0]
    assert seq % ROW_TILE == 0 and seq % TQ == 0 and TK % TQ == 0 and seq // 4 >= TOPK
    cos_t, sin_t = _rope_tables(positions)
    x2 = x.reshape(batch * seq, D_MODEL)
    c_q = POOL_WIDTH
    c_v = POOL_WIDTH + 2 * ATTN_WIDTH
    c_iq = c_v + ATTN_WIDTH
    c_ik = c_iq + IDX_HEADS * IDX_DIM
    c_iw = c_ik + IDX_DIM
    for l in range(depth):
        wl = w_in[l]
        w_ik = wl[:, c_ik:c_iw]
        w_main = jnp.concatenate([wl[:, :c_v], wl[:, c_iq:c_ik], w_ik, w_ik], axis=1).astype(BF16)
        wvt = wl[:, c_v:c_iq].T.astype(BF16)
        wiwt = jnp.pad(wl[:, c_iw:].T, ((0, IW_ROWS - IDX_HEADS), (0, 0))).astype(BF16)
        a, q, k, vt, iq, ikk, iwt = _proj_call(
            x2, g_pre_mix[l][None], w_main, wvt, wiwt, cos_t, sin_t, w_pool[l].astype(BF16),
            pool_scale[l][None], seq=seq)
        b = _attn_call(q, k, vt, iq, ikk, iwt, batch=batch, seq=seq)
        wo = w_out[l].astype(BF16)
        x2 = _post_call(x2, a, b, wo[:POOL_WIDTH], wo[POOL_WIDTH:], g_post_mix[l][None], g_pre_ffn[l][None],
                        w_ff1[l].astype(BF16), w_ff2[l].astype(BF16), g_post_ffn[l][None])
    del c_q
    return x2.reshape(batch, seq, D_MODEL)
```

```python
import functools

import jax
import jax.numpy as jnp
from jax import lax
from jax.experimental import pallas as pl
from jax.experimental.pallas import tpu as pltpu

F32 = jnp.float32
BF16 = jnp.bfloat16
I32 = jnp.int32
I16 = jnp.int16

D_MODEL = 1024
CHUNK_SHIFT = 6
POOL_WIDTH = 512
POOL_WINDOWS = (2, 4, 8, 16)
POOL_GROUP = 128
POOL_HALO = 16
ATTN_WIDTH = 512
HEAD_DIM = 64
N_HEADS = 8
N_PAIRS = N_HEADS // 2
ROPE_HALF = 8
ROPE_THETA = 500000.0
IDX_HEADS = 8
IDX_DIM = 64
TOPK = 256
D_FF = 4096
EPS = 1e-6
LANES = 128
IW_ROWS = 16

ROW_TILE = 512
TQ = 256
TK = ROW_TILE
SCORE_ROWS = 64
FF_CHUNK = 1024
MASK_NEG = -1e30
LOG2_E = 1.4426950408889634
PACK_ROWS = 16
COUNT_CHAINS = 4
I16_MIN, I16_MAX = -32768, 32767
F32_LOWEST = -3.4028234663852886e38

NT_DIMS = (((1,), (1,)), ((), ()))


def _rms(x, g):
    return x * lax.rsqrt(jnp.mean(x * x, axis=-1, keepdims=True) + EPS) * g


def _tree(fn, xs):
    xs = list(xs)
    while len(xs) > 1:
        nxt = [fn(xs[i], xs[i + 1]) for i in range(0, len(xs) - 1, 2)]
        if len(xs) % 2:
            nxt.append(xs[-1])
        xs = nxt
    return xs[0]


def _rows8(fn, x):
    return _tree(fn, [x[i:i + 8] for i in range(0, x.shape[0], 8)])


def _proj_kernel(x_ref, g_ref, w_ref, wvt_ref, wiwt_ref, cos_ref, sin_ref, wpool_ref, pscale_ref,
                 a_ref, q_ref, k_ref, vt_ref, iq_ref, ikk_ref, iwt_ref, ubuf_ref, *, tiles_per_seq):
    tm = x_ref.shape[0]
    t_in_seq = pl.program_id(0) % tiles_per_seq
    h = _rms(x_ref[...], g_ref[...]).astype(BF16)

    u = jnp.dot(h, w_ref[:, 0:POOL_WIDTH], preferred_element_type=F32)

    @pl.when(t_in_seq == 0)
    def _():
        ubuf_ref[0:POOL_HALO, :] = jnp.zeros((POOL_HALO, POOL_WIDTH), F32)

    @pl.when(t_in_seq != 0)
    def _():
        ubuf_ref[0:POOL_HALO, :] = ubuf_ref[tm:tm + POOL_HALO, :]

    ubuf_ref[POOL_HALO:POOL_HALO + tm, :] = u
    pos = t_in_seq * tm + lax.broadcasted_iota(I32, (tm, 1), 0)
    for g, w in enumerate(POOL_WINDOWS):
        c0 = g * POOL_GROUP
        ug = u[:, c0:c0 + POOL_GROUP]
        ssum = ug
        for j in range(1, w):
            ssum = ssum + ubuf_ref[POOL_HALO - j:POOL_HALO - j + tm, c0:c0 + POOL_GROUP]
        cnt = jnp.minimum(pos + 1, w).astype(F32)
        d = (ssum / cnt - ug).astype(BF16)
        y = jnp.dot(d, wpool_ref[g], preferred_element_type=F32) * pscale_ref[:, c0:c0 + POOL_GROUP]
        a_ref[:, c0:c0 + POOL_GROUP] = y.astype(BF16)

    cos_t = cos_ref[...]
    sin_t = sin_ref[...]
    first_half = (lax.broadcasted_iota(I32, (tm, LANES), 1) % HEAD_DIM) < ROPE_HALF

    def rope(z):
        partner = jnp.where(first_half, pltpu.roll(z, LANES - ROPE_HALF, 1), pltpu.roll(z, ROPE_HALF, 1))
        return z * cos_t + partner * sin_t

    def head_group(col0, out_ref, scale):
        z = jnp.dot(h, w_ref[:, col0:col0 + ATTN_WIDTH], preferred_element_type=F32)
        for j in range(N_PAIRS):
            r = rope(z[:, j * LANES:(j + 1) * LANES])
            if scale != 1.0:
                r = r * scale
            out_ref[j] = r.astype(BF16)

    head_group(POOL_WIDTH, q_ref, HEAD_DIM ** -0.5 * LOG2_E)
    head_group(POOL_WIDTH + ATTN_WIDTH, k_ref, 1.0)
    head_group(POOL_WIDTH + 2 * ATTN_WIDTH, iq_ref, IDX_DIM ** -0.5)
    col_ik = POOL_WIDTH + 3 * ATTN_WIDTH
    ikk = jnp.dot(h, w_ref[:, col_ik:col_ik + LANES], preferred_element_type=F32)
    ikk_ref[...] = rope(ikk).astype(BF16)

    vt_ref[0] = lax.dot_general(wvt_ref[...], h, NT_DIMS, preferred_element_type=F32).astype(BF16)
    iwt_ref[...] = lax.dot_general(wiwt_ref[...], h, NT_DIMS, preferred_element_type=F32) * (IDX_HEADS ** -0.5)


def _const_spec(shape):
    return pl.BlockSpec(shape, lambda *_: (0,) * len(shape))


def _proj_call(x2, g, w_main, wvt, wiwt, cos_t, sin_t, wpool, pscale, *, seq):
    n = x2.shape[0]
    tm = ROW_TILE
    tps = seq // tm
    row = lambda width: pl.BlockSpec((tm, width), lambda i: (i, 0))
    pair = pl.BlockSpec((N_PAIRS, tm, LANES), lambda i: (0, i, 0))
    pair_shape = jax.ShapeDtypeStruct((N_PAIRS, n, LANES), BF16)
    return pl.pallas_call(
        functools.partial(_proj_kernel, tiles_per_seq=tps),
        grid=(n // tm,),
        in_specs=[row(D_MODEL), _const_spec((1, D_MODEL)), _const_spec(w_main.shape), _const_spec(wvt.shape),
                  _const_spec(wiwt.shape), row(LANES), row(LANES), _const_spec(wpool.shape),
                  _const_spec((1, POOL_WIDTH))],
        out_specs=[row(POOL_WIDTH), pair, pair,
                   pl.BlockSpec((1, ATTN_WIDTH, tm), lambda i: (i, 0, 0)),
                   pair, row(LANES), pl.BlockSpec((IW_ROWS, tm), lambda i: (0, i))],
        out_shape=[jax.ShapeDtypeStruct((n, POOL_WIDTH), BF16), pair_shape, pair_shape,
                   jax.ShapeDtypeStruct((n // tm, ATTN_WIDTH, tm), BF16),
                   pair_shape, jax.ShapeDtypeStruct((n, LANES), BF16),
                   jax.ShapeDtypeStruct((IW_ROWS, n), F32)],
        scratch_shapes=[pltpu.VMEM((POOL_HALO + tm, POOL_WIDTH), F32)],
        compiler_params=pltpu.CompilerParams(dimension_semantics=("arbitrary",)),
        name="proj",
    )(x2, g, w_main, wvt, wiwt, cos_t, sin_t, wpool, pscale)


def _key_of(x):
    b = lax.bitcast_convert_type(x, I32)
    return b ^ ((b >> 31) & 0x7FFFFFFF)


def _val_of(key):
    return lax.bitcast_convert_type(key ^ ((key >> 31) & 0x7FFFFFFF), F32)


def _attn_kernel(q_ref, k_ref, vt_ref, iq_ref, ikk_ref, iwt_ref, o_ref,
                 qs_ref, iqs_ref, st_ref, khi_ref, klo_ref, lg_ref, m_ref, ls_ref, ot_ref,
                 s_ref, p_ref):
    tq = q_ref.shape[1]
    tk = vt_ref.shape[2]
    q0 = pl.program_id(1) * tq
    n_kt = (q0 + tq + tk - 1) // tk
    q_chunk = (q0 + lax.broadcasted_iota(I32, (1, tq), 1)) >> CHUNK_SHIFT

    low = lax.broadcasted_iota(I32, (tq, LANES), 1) < HEAD_DIM
    for j in range(N_PAIRS):
        qp = q_ref[j].astype(F32)
        qs_ref[2 * j] = jnp.where(low, qp, 0.0).astype(BF16)
        qs_ref[2 * j + 1] = jnp.where(low, 0.0, qp).astype(BF16)
        ip = iq_ref[j].astype(F32)
        iqs_ref[2 * j * tq:(2 * j + 1) * tq, :] = jnp.where(low, ip, 0.0).astype(BF16)
        iqs_ref[(2 * j + 1) * tq:(2 * j + 2) * tq, :] = jnp.where(low, 0.0, ip).astype(BF16)

    def score_tile(kt, masked):
        r0 = pl.multiple_of(kt * tk, tk)
        lg_ref[...] = lax.dot_general(ikk_ref[pl.ds(r0, tk), :], iqs_ref[...], NT_DIMS,
                                      preferred_element_type=F32)
        for r in range(tk // SCORE_ROWS):
            rows = slice(r * SCORE_ROWS, (r + 1) * SCORE_ROWS)
            sc = None
            for hh in range(IDX_HEADS):
                c = iwt_ref[hh:hh + 1, :] * jnp.maximum(lg_ref[rows, hh * tq:(hh + 1) * tq], 0.0)
                sc = c if sc is None else sc + c
            sc = jnp.where(sc == 0.0, 0.0, sc)
            if masked:
                k_idx = r0 + r * SCORE_ROWS + lax.broadcasted_iota(I32, (SCORE_ROWS, tq), 0)
                allowed = (k_idx >> CHUNK_SHIFT) <= q_chunk
                sc = jnp.where(allowed, sc, -jnp.inf)
            out_rows = pl.ds(r0 + r * SCORE_ROWS, SCORE_ROWS)
            st_ref[out_rows, :] = sc
            key = _key_of(sc)
            khi_ref[out_rows, :] = (key >> 16).astype(I16)
            klo_ref[out_rows, :] = ((key & 0xFFFF) - 0x8000).astype(I16)

    def score_body(kt, c):
        score_tile(kt, False)
        return c

    lax.fori_loop(0, n_kt - 1, score_body, 0)
    score_tile(n_kt - 1, True)

    def count16(src_ref, cand):
        c16 = cand.astype(I16)

        def body(kt, accs):
            blk = src_ref[pl.ds(pl.multiple_of(kt * tk, tk), tk), :]
            ind = jnp.where(blk >= c16, jnp.int16(1), jnp.int16(0))
            parts = [ind[i:i + PACK_ROWS] for i in range(0, tk, PACK_ROWS)]
            return tuple(a + _tree(jnp.add, parts[j::len(accs)]) for j, a in enumerate(accs))

        zero = jnp.zeros((PACK_ROWS, tq), I16)
        accs = lax.fori_loop(0, n_kt, body, (zero,) * COUNT_CHAINS)
        return jnp.sum(_tree(jnp.add, list(accs)).astype(I32), axis=0, keepdims=True)

    def radix_search(src_ref, need, count_all):
        def body(i, state):
            t, c_t = state
            cand = t + jnp.left_shift(jnp.int32(1), 15 - i)
            c = count16(src_ref, cand)
            ok = c >= need
            return jnp.where(ok, cand, t), jnp.where(ok, c, c_t)
        return lax.fori_loop(0, 16, body, (jnp.full((1, tq), I16_MIN, I32), count_all))

    n_allowed = (q_chunk + 1) << CHUNK_SHIFT
    n_stored = jnp.full((1, tq), n_kt * tk, I32)
    t_hi, _ = radix_search(khi_ref, TOPK, n_stored)
    c_above = jnp.where(t_hi >= I16_MAX, 0, count16(khi_ref, jnp.minimum(t_hi + 1, I16_MAX)))
    t_hi16 = t_hi.astype(I16)

    def keep_bucket(kt, c):
        rows = pl.ds(pl.multiple_of(kt * tk, tk), tk)
        klo_ref[rows, :] = jnp.where(khi_ref[rows, :] == t_hi16, klo_ref[rows, :], jnp.int16(I16_MIN))
        return c

    lax.fori_loop(0, n_kt, keep_bucket, 0)
    t_lo, c_bucket = radix_search(klo_ref, TOPK - c_above, n_stored)
    searched = n_allowed > TOPK
    thr_key = (t_hi << 16) | (t_lo + 0x8000)
    thr = jnp.where(searched, _val_of(thr_key), F32_LOWEST)
    c_lo = jnp.where(searched, c_above + c_bucket, n_allowed).astype(F32)

    has_tie = jnp.max(jnp.where(c_lo > TOPK, 1.0, 0.0)) > 0.5

    @pl.when(jnp.logical_not(has_tie))
    def _():
        def body(kt, c):
            rows = pl.ds(pl.multiple_of(kt * tk, tk), tk)
            st_ref[rows, :] = jnp.where(st_ref[rows, :] >= thr, 0.0, MASK_NEG)
            return c
        lax.fori_loop(0, n_kt, body, 0)

    @pl.when(has_tie)
    def _():
        def count_above(kt, acc):
            blk = st_ref[pl.ds(pl.multiple_of(kt * tk, tk), tk), :]
            return acc + _rows8(jnp.add, jnp.where(blk > thr, 1.0, 0.0))
        above = lax.fori_loop(0, n_kt, count_above, jnp.zeros((8, tq), F32))
        budget = TOPK - jnp.sum(above, axis=0, keepdims=True)
        ri = lax.broadcasted_iota(I32, (tk, tk), 0)
        ci = lax.broadcasted_iota(I32, (tk, tk), 1)
        prefix = jnp.where(ci <= ri, 1.0, 0.0).astype(BF16)

        def body(kt, seen):
            rows = pl.ds(pl.multiple_of(kt * tk, tk), tk)
            blk = st_ref[rows, :]
            eq = blk == thr
            rank = jnp.dot(prefix, jnp.where(eq, 1.0, 0.0).astype(BF16), preferred_element_type=F32) + seen
            keep = (blk > thr) | (eq & (rank <= budget))
            st_ref[rows, :] = jnp.where(keep, 0.0, MASK_NEG)
            return rank[tk - 1:tk, :]
        lax.fori_loop(0, n_kt, body, jnp.zeros((1, tq), F32))

    m_ref[...] = jnp.full(m_ref.shape, -jnp.inf, F32)
    ls_ref[...] = jnp.zeros(ls_ref.shape, F32)
    ot_ref[...] = jnp.zeros(ot_ref.shape, F32)
    n_sub = tk // SCORE_ROWS
    ones_rows = jnp.ones((PACK_ROWS, tk), BF16)

    def stage_logits(kt, hh):
        r0 = pl.multiple_of(kt * tk, tk)
        s_ref[hh] = lax.dot_general(k_ref[hh // 2, pl.ds(r0, tk), :], qs_ref[hh], NT_DIMS,
                                    preferred_element_type=F32)

    for hh in range(N_HEADS):
        stage_logits(0, hh)

    def kt_body(kt, c):
        r0 = pl.multiple_of(kt * tk, tk)
        kt_next = jnp.minimum(kt + 1, n_kt - 1)

        for hh in range(N_HEADS):
            feat = slice(hh * HEAD_DIM, (hh + 1) * HEAD_DIM)
            m_prev = m_ref[hh]
            maxes = []
            for r in range(n_sub):
                rows = slice(r * SCORE_ROWS, (r + 1) * SCORE_ROWS)
                t = s_ref[hh, rows, :] + st_ref[pl.ds(r0 + r * SCORE_ROWS, SCORE_ROWS), :]
                s_ref[hh, rows, :] = t
                maxes.append(_rows8(jnp.maximum, t))
            mx = _tree(jnp.maximum, maxes)
            m_new = jnp.maximum(m_prev, jnp.max(mx, axis=0, keepdims=True))
            m_ref[hh] = m_new
            alpha = jnp.exp2(m_prev - m_new)
            for r in range(n_sub):
                rows = slice(r * SCORE_ROWS, (r + 1) * SCORE_ROWS)
                p_ref[hh, rows, :] = jnp.exp2(s_ref[hh, rows, :] - m_new).astype(BF16)
            v_ext = jnp.concatenate([vt_ref[kt, feat, :], ones_rows], axis=0)
            pv = jnp.dot(v_ext, p_ref[hh], preferred_element_type=F32)
            ls_ref[hh] = alpha * ls_ref[hh] + pv[HEAD_DIM:HEAD_DIM + 1, :]
            ot_ref[feat, :] = alpha * ot_ref[feat, :] + pv[0:HEAD_DIM, :]
            stage_logits(kt_next, hh)
        return c

    lax.fori_loop(0, n_kt, kt_body, 0)
    for hh in range(N_HEADS):
        feat = slice(hh * HEAD_DIM, (hh + 1) * HEAD_DIM)
        ot_ref[feat, :] = ot_ref[feat, :] / ls_ref[hh]
    o_ref[...] = ot_ref[...].T.astype(BF16)


def _attn_call(q, k, vt, iq, ikk, iwt, *, batch, seq):
    n = batch * seq
    nq = seq // TQ
    n_vt = seq // TK
    qspec = pl.BlockSpec((N_PAIRS, TQ, LANES), lambda b, i: (0, b * nq + i, 0))
    return pl.pallas_call(
        _attn_kernel,
        grid=(batch, nq),
        in_specs=[qspec,
                  pl.BlockSpec((N_PAIRS, seq, LANES), lambda b, i: (0, b, 0)),
                  pl.BlockSpec((n_vt, ATTN_WIDTH, TK), lambda b, i: (b, 0, 0)),
                  qspec,
                  pl.BlockSpec((seq, LANES), lambda b, i: (b, 0)),
                  pl.BlockSpec((IW_ROWS, TQ), lambda b, i: (0, b * nq + i))],
        out_specs=pl.BlockSpec((TQ, ATTN_WIDTH), lambda b, i: (b * nq + i, 0)),
        out_shape=jax.ShapeDtypeStruct((n, ATTN_WIDTH), BF16),
        scratch_shapes=[pltpu.VMEM((N_HEADS, TQ, LANES), BF16),
                        pltpu.VMEM((IDX_HEADS * TQ, LANES), BF16),
                        pltpu.VMEM((seq, TQ), F32),
                        pltpu.VMEM((seq, TQ), I16), pltpu.VMEM((seq, TQ), I16),
                        pltpu.VMEM((TK, IDX_HEADS * TQ), F32),
                        pltpu.VMEM((N_HEADS, 1, TQ), F32), pltpu.VMEM((N_HEADS, 1, TQ), F32),
                        pltpu.VMEM((ATTN_WIDTH, TQ), F32),
                        pltpu.VMEM((N_HEADS, TK, TQ), F32),
                        pltpu.VMEM((N_HEADS, TK, TQ), BF16)],
        compiler_params=pltpu.CompilerParams(dimension_semantics=("arbitrary", "arbitrary")),
        name="attn",
    )(q, k, vt, iq, ikk, iwt)


def _post_kernel(x_ref, a_ref, b_ref, woa_ref, wob_ref, gmix_ref, gffn_ref, w1_ref, w2_ref, gout_ref, o_ref):
    mix = (jnp.dot(a_ref[...], woa_ref[...], preferred_element_type=F32)
           + jnp.dot(b_ref[...], wob_ref[...], preferred_element_type=F32))
    x1 = x_ref[...] + _rms(mix, gmix_ref[...])
    h = _rms(x1, gffn_ref[...]).astype(BF16)
    f = None
    for c in range(D_FF // FF_CHUNK):
        cols = slice(c * FF_CHUNK, (c + 1) * FF_CHUNK)
        f1 = jnp.square(jnp.maximum(jnp.dot(h, w1_ref[:, cols], preferred_element_type=F32), 0.0)).astype(BF16)
        part = jnp.dot(f1, w2_ref[cols, :], preferred_element_type=F32)
        f = part if f is None else f + part
    o_ref[...] = x1 + _rms(f, gout_ref[...])


def _post_call(x2, a, b, woa, wob, gmix, gffn, w1, w2, gout):
    n = x2.shape[0]
    tm = ROW_TILE
    row = lambda width: pl.BlockSpec((tm, width), lambda i: (i, 0))
    gspec = _const_spec((1, D_MODEL))
    return pl.pallas_call(
        _post_kernel,
        grid=(n // tm,),
        in_specs=[row(D_MODEL), row(POOL_WIDTH), row(ATTN_WIDTH), _const_spec(woa.shape), _const_spec(wob.shape),
                  gspec, gspec, _const_spec(w1.shape), _const_spec(w2.shape), gspec],
        out_specs=row(D_MODEL),
        out_shape=jax.ShapeDtypeStruct((n, D_MODEL), F32),
        compiler_params=pltpu.CompilerParams(dimension_semantics=("arbitrary",),
                                             vmem_limit_bytes=56 * 1024 * 1024),
        name="post",
    )(x2, a, b, woa, wob, gmix, gffn, w1, w2, gout)


def _rope_tables(positions):
    inv = ROPE_THETA ** (-jnp.arange(ROPE_HALF, dtype=F32) / ROPE_HALF)
    ang = positions.astype(F32).reshape(-1, 1) * inv
    cos, sin = jnp.cos(ang), jnp.sin(ang)
    rest = HEAD_DIM - 2 * ROPE_HALF
    cos64 = jnp.concatenate([cos, cos, jnp.ones((cos.shape[0], rest), F32)], axis=-1)
    sin64 = jnp.concatenate([-sin, sin, jnp.zeros((sin.shape[0], rest), F32)], axis=-1)
    return jnp.tile(cos64, (1, LANES // HEAD_DIM)), jnp.tile(sin64, (1, LANES // HEAD_DIM))


def kernel(x, positions, g_pre_mix, w_in, w_pool, pool_scale, w_out, g_post_mix, g_pre_ffn, w_ff1, w_ff2,
           g_post_ffn):
    batch, seq, _ = x.shape
    depth = w_in.shape[0]
    assert seq % ROW_TILE == 0 and seq % TQ == 0 and TK % TQ == 0 and seq // 4 >= TOPK
    cos_t, sin_t = _rope_tables(positions)
    x2 = x.reshape(batch * seq, D_MODEL)
    c_v = POOL_WIDTH + 2 * ATTN_WIDTH
    c_iq = c_v + ATTN_WIDTH
    c_ik = c_iq + IDX_HEADS * IDX_DIM
    c_iw = c_ik + IDX_DIM
    for l in range(depth):
        wl = w_in[l]
        w_ik = wl[:, c_ik:c_iw]
        w_main = jnp.concatenate([wl[:, :c_v], wl[:, c_iq:c_ik], w_ik, w_ik], axis=1).astype(BF16)
        wvt = wl[:, c_v:c_iq].T.astype(BF16)
        wiwt = jnp.pad(wl[:, c_iw:].T, ((0, IW_ROWS - IDX_HEADS), (0, 0))).astype(BF16)
        a, q, k, vt, iq, ikk, iwt = _proj_call(
            x2, g_pre_mix[l][None], w_main, wvt, wiwt, cos_t, sin_t, w_pool[l].astype(BF16),
            pool_scale[l][None], seq=seq)
        b = _attn_call(q, k, vt, iq, ikk, iwt, batch=batch, seq=seq)
        wo = w_out[l].astype(BF16)
        x2 = _post_call(x2, a, b, wo[:POOL_WIDTH], wo[POOL_WIDTH:], g_post_mix[l][None], g_pre_ffn[l][None],
                        w_ff1[l].astype(BF16), w_ff2[l].astype(BF16), g_post_ffn[l][None])
    return x2.reshape(batch, seq, D_MODEL)
```

```python
import functools

import jax
import jax.numpy as jnp
from jax import lax
from jax.experimental import pallas as pl
from jax.experimental.pallas import tpu as pltpu

F32 = jnp.float32
BF16 = jnp.bfloat16
I32 = jnp.int32
I16 = jnp.int16

D_MODEL = 1024
CHUNK_SHIFT = 6
POOL_WIDTH = 512
POOL_WINDOWS = (2, 4, 8, 16)
POOL_GROUP = 128
POOL_HALO = 16
ATTN_WIDTH = 512
HEAD_DIM = 64
N_HEADS = 8
N_PAIRS = N_HEADS // 2
ROPE_HALF = 8
ROPE_THETA = 500000.0
IDX_HEADS = 8
IDX_DIM = 64
TOPK = 256
D_FF = 4096
EPS = 1e-6
LANES = 128
IW_ROWS = 16

ROW_TILE = 512
TQ = 256
TK = ROW_TILE
SCORE_ROWS = 64
FF_CHUNK = 1024
MASK_NEG = -1e30
LOG2_E = 1.4426950408889634
LOGITS_AHEAD = 2
PACK_ROWS = 16
COUNT_CHAINS = 4
I16_MIN, I16_MAX = -32768, 32767
F32_LOWEST = -3.4028234663852886e38

NT_DIMS = (((1,), (1,)), ((), ()))


def _rms(x, g):
    return x * lax.rsqrt(jnp.mean(x * x, axis=-1, keepdims=True) + EPS) * g


def _tree(fn, xs):
    xs = list(xs)
    while len(xs) > 1:
        nxt = [fn(xs[i], xs[i + 1]) for i in range(0, len(xs) - 1, 2)]
        if len(xs) % 2:
            nxt.append(xs[-1])
        xs = nxt
    return xs[0]


def _rows8(fn, x):
    return _tree(fn, [x[i:i + 8] for i in range(0, x.shape[0], 8)])


def _proj_kernel(x_ref, g_ref, w_ref, wvt_ref, wiwt_ref, cos_ref, sin_ref, wpool_ref, pscale_ref,
                 a_ref, q_ref, k_ref, vt_ref, iq_ref, ikk_ref, iwt_ref, ubuf_ref, *, tiles_per_seq):
    tm = x_ref.shape[0]
    t_in_seq = pl.program_id(0) % tiles_per_seq
    h = _rms(x_ref[...], g_ref[...]).astype(BF16)

    u = jnp.dot(h, w_ref[:, 0:POOL_WIDTH], preferred_element_type=F32)

    @pl.when(t_in_seq == 0)
    def _():
        ubuf_ref[0:POOL_HALO, :] = jnp.zeros((POOL_HALO, POOL_WIDTH), F32)

    @pl.when(t_in_seq != 0)
    def _():
        ubuf_ref[0:POOL_HALO, :] = ubuf_ref[tm:tm + POOL_HALO, :]

    ubuf_ref[POOL_HALO:POOL_HALO + tm, :] = u
    pos = t_in_seq * tm + lax.broadcasted_iota(I32, (tm, 1), 0)
    for g, w in enumerate(POOL_WINDOWS):
        c0 = g * POOL_GROUP
        ug = u[:, c0:c0 + POOL_GROUP]
        ssum = ug
        for j in range(1, w):
            ssum = ssum + ubuf_ref[POOL_HALO - j:POOL_HALO - j + tm, c0:c0 + POOL_GROUP]
        cnt = jnp.minimum(pos + 1, w).astype(F32)
        d = (ssum / cnt - ug).astype(BF16)
        y = jnp.dot(d, wpool_ref[g], preferred_element_type=F32) * pscale_ref[:, c0:c0 + POOL_GROUP]
        a_ref[:, c0:c0 + POOL_GROUP] = y.astype(BF16)

    cos_t = cos_ref[...]
    sin_t = sin_ref[...]
    first_half = (lax.broadcasted_iota(I32, (tm, LANES), 1) % HEAD_DIM) < ROPE_HALF

    def rope(z):
        partner = jnp.where(first_half, pltpu.roll(z, LANES - ROPE_HALF, 1), pltpu.roll(z, ROPE_HALF, 1))
        return z * cos_t + partner * sin_t

    def head_group(col0, out_ref, scale):
        z = jnp.dot(h, w_ref[:, col0:col0 + ATTN_WIDTH], preferred_element_type=F32)
        for j in range(N_PAIRS):
            r = rope(z[:, j * LANES:(j + 1) * LANES])
            if scale != 1.0:
                r = r * scale
            out_ref[j] = r.astype(BF16)

    head_group(POOL_WIDTH, q_ref, HEAD_DIM ** -0.5 * LOG2_E)
    head_group(POOL_WIDTH + ATTN_WIDTH, k_ref, 1.0)
    head_group(POOL_WIDTH + 2 * ATTN_WIDTH, iq_ref, IDX_DIM ** -0.5)
    col_ik = POOL_WIDTH + 3 * ATTN_WIDTH
    ikk = jnp.dot(h, w_ref[:, col_ik:col_ik + LANES], preferred_element_type=F32)
    ikk_ref[...] = rope(ikk).astype(BF16)

    vt_ref[0] = lax.dot_general(wvt_ref[...], h, NT_DIMS, preferred_element_type=F32).astype(BF16)
    iwt_ref[...] = lax.dot_general(wiwt_ref[...], h, NT_DIMS, preferred_element_type=F32) * (IDX_HEADS ** -0.5)


def _const_spec(shape):
    return pl.BlockSpec(shape, lambda *_: (0,) * len(shape))


def _proj_call(x2, g, w_main, wvt, wiwt, cos_t, sin_t, wpool, pscale, *, seq):
    n = x2.shape[0]
    tm = ROW_TILE
    tps = seq // tm
    row = lambda width: pl.BlockSpec((tm, width), lambda i: (i, 0))
    pair = pl.BlockSpec((N_PAIRS, tm, LANES), lambda i: (0, i, 0))
    pair_shape = jax.ShapeDtypeStruct((N_PAIRS, n, LANES), BF16)
    return pl.pallas_call(
        functools.partial(_proj_kernel, tiles_per_seq=tps),
        grid=(n // tm,),
        in_specs=[row(D_MODEL), _const_spec((1, D_MODEL)), _const_spec(w_main.shape), _const_spec(wvt.shape),
                  _const_spec(wiwt.shape), row(LANES), row(LANES), _const_spec(wpool.shape),
                  _const_spec((1, POOL_WIDTH))],
        out_specs=[row(POOL_WIDTH), pair, pair,
                   pl.BlockSpec((1, ATTN_WIDTH, tm), lambda i: (i, 0, 0)),
                   pair, row(LANES), pl.BlockSpec((IW_ROWS, tm), lambda i: (0, i))],
        out_shape=[jax.ShapeDtypeStruct((n, POOL_WIDTH), BF16), pair_shape, pair_shape,
                   jax.ShapeDtypeStruct((n // tm, ATTN_WIDTH, tm), BF16),
                   pair_shape, jax.ShapeDtypeStruct((n, LANES), BF16),
                   jax.ShapeDtypeStruct((IW_ROWS, n), F32)],
        scratch_shapes=[pltpu.VMEM((POOL_HALO + tm, POOL_WIDTH), F32)],
        compiler_params=pltpu.CompilerParams(dimension_semantics=("arbitrary",)),
        name="proj",
    )(x2, g, w_main, wvt, wiwt, cos_t, sin_t, wpool, pscale)


def _key_of(x):
    b = lax.bitcast_convert_type(x, I32)
    sign = b >> 31
    return (b ^ (sign & 0x7FFFFFFF)) - sign


def _val_of(key):
    k = key + (key >> 31)
    return lax.bitcast_convert_type(k ^ ((k >> 31) & 0x7FFFFFFF), F32)


def _attn_kernel(q_ref, k_ref, vt_ref, iq_ref, ikk_ref, iwt_ref, o_ref,
                 qs_ref, iqs_ref, st_ref, khi_ref, klo_ref, lg_ref, m_ref, ls_ref, ot_ref,
                 s_ref, t_ref, p_ref):
    tq = q_ref.shape[1]
    tk = vt_ref.shape[2]
    q0 = pl.program_id(1) * tq
    n_kt = (q0 + tq + tk - 1) // tk
    q_chunk = (q0 + lax.broadcasted_iota(I32, (1, tq), 1)) >> CHUNK_SHIFT

    low = lax.broadcasted_iota(I32, (tq, LANES), 1) < HEAD_DIM
    for j in range(N_PAIRS):
        qp = q_ref[j].astype(F32)
        qs_ref[2 * j] = jnp.where(low, qp, 0.0).astype(BF16)
        qs_ref[2 * j + 1] = jnp.where(low, 0.0, qp).astype(BF16)
        ip = iq_ref[j].astype(F32)
        iqs_ref[2 * j * tq:(2 * j + 1) * tq, :] = jnp.where(low, ip, 0.0).astype(BF16)
        iqs_ref[(2 * j + 1) * tq:(2 * j + 2) * tq, :] = jnp.where(low, 0.0, ip).astype(BF16)

    def score_tile(kt, masked):
        r0 = pl.multiple_of(kt * tk, tk)
        lg_ref[...] = lax.dot_general(ikk_ref[pl.ds(r0, tk), :], iqs_ref[...], NT_DIMS,
                                      preferred_element_type=F32)
        for r in range(tk // SCORE_ROWS):
            rows = slice(r * SCORE_ROWS, (r + 1) * SCORE_ROWS)
            sc = None
            for hh in range(IDX_HEADS):
                c = iwt_ref[hh:hh + 1, :] * jnp.maximum(lg_ref[rows, hh * tq:(hh + 1) * tq], 0.0)
                sc = c if sc is None else sc + c
            if masked:
                k_idx = r0 + r * SCORE_ROWS + lax.broadcasted_iota(I32, (SCORE_ROWS, tq), 0)
                allowed = (k_idx >> CHUNK_SHIFT) <= q_chunk
                sc = jnp.where(allowed, sc, -jnp.inf)
            out_rows = pl.ds(r0 + r * SCORE_ROWS, SCORE_ROWS)
            st_ref[out_rows, :] = sc
            key = _key_of(sc)
            khi_ref[out_rows, :] = (key >> 16).astype(I16)
            klo_ref[out_rows, :] = (key ^ 0x8000).astype(I16)

    def score_body(kt, c):
        score_tile(kt, False)
        return c

    lax.fori_loop(0, n_kt - 1, score_body, 0)
    score_tile(n_kt - 1, True)

    def count16(src_ref, cand):
        c16 = cand.astype(I16)

        def body(kt, accs):
            blk = src_ref[pl.ds(pl.multiple_of(kt * tk, tk), tk), :]
            ind = jnp.where(blk >= c16, jnp.int16(1), jnp.int16(0))
            parts = [ind[i:i + PACK_ROWS] for i in range(0, tk, PACK_ROWS)]
            return tuple(a + _tree(jnp.add, parts[j::len(accs)]) for j, a in enumerate(accs))

        zero = jnp.zeros((PACK_ROWS, tq), I16)
        accs = lax.fori_loop(0, n_kt, body, (zero,) * COUNT_CHAINS)
        return jnp.sum(_tree(jnp.add, list(accs)).astype(I32), axis=0, keepdims=True)

    n_allowed = (q_chunk + 1) << CHUNK_SHIFT
    searched = n_allowed > TOPK
    n_stored = jnp.full((1, tq), n_kt * tk, I32)

    def radix_step(src_ref, need, bit, state):
        t, c_t, c_refused = state
        cand = t + jnp.left_shift(jnp.int32(1), bit)
        c = count16(src_ref, cand)
        ok = c >= need
        return jnp.where(ok, cand, t), jnp.where(ok, c, c_t), jnp.where(ok, c_refused, c)

    start = (jnp.full((1, tq), I16_MIN, I32), n_stored, jnp.zeros((1, tq), I32))
    t_hi, _, c_above = lax.fori_loop(0, 16, lambda i, st: radix_step(khi_ref, TOPK, 15 - i, st), start)
    t_hi16 = t_hi.astype(I16)

    def keep_bucket(kt, c):
        rows = pl.ds(pl.multiple_of(kt * tk, tk), tk)
        klo_ref[rows, :] = jnp.where(khi_ref[rows, :] == t_hi16, klo_ref[rows, :], jnp.int16(I16_MIN))
        return c

    lax.fori_loop(0, n_kt, keep_bucket, 0)

    t_lo, c_bucket, _ = lax.fori_loop(
        0, 16, lambda i, st: radix_step(klo_ref, TOPK - c_above, 15 - i, st), start)
    thr_key = (t_hi << 16) | (t_lo + 0x8000)
    thr = jnp.where(searched, _val_of(thr_key), F32_LOWEST)
    c_lo = jnp.where(searched, c_above + c_bucket, n_allowed).astype(F32)

    has_tie = jnp.max(jnp.where(c_lo > TOPK, 1.0, 0.0)) > 0.5

    @pl.when(jnp.logical_not(has_tie))
    def _():
        def body(kt, c):
            rows = pl.ds(pl.multiple_of(kt * tk, tk), tk)
            st_ref[rows, :] = jnp.where(st_ref[rows, :] >= thr, 0.0, MASK_NEG)
            return c
        lax.fori_loop(0, n_kt, body, 0)

    @pl.when(has_tie)
    def _():
        def count_above(kt, acc):
            blk = st_ref[pl.ds(pl.multiple_of(kt * tk, tk), tk), :]
            return acc + _rows8(jnp.add, jnp.where(blk > thr, 1.0, 0.0))
        above = lax.fori_loop(0, n_kt, count_above, jnp.zeros((8, tq), F32))
        budget = TOPK - jnp.sum(above, axis=0, keepdims=True)
        ri = lax.broadcasted_iota(I32, (tk, tk), 0)
        ci = lax.broadcasted_iota(I32, (tk, tk), 1)
        prefix = jnp.where(ci <= ri, 1.0, 0.0).astype(BF16)

        def body(kt, seen):
            rows = pl.ds(pl.multiple_of(kt * tk, tk), tk)
            blk = st_ref[rows, :]
            eq = blk == thr
            rank = jnp.dot(prefix, jnp.where(eq, 1.0, 0.0).astype(BF16), preferred_element_type=F32) + seen
            keep = (blk > thr) | (eq & (rank <= budget))
            st_ref[rows, :] = jnp.where(keep, 0.0, MASK_NEG)
            return rank[tk - 1:tk, :]
        lax.fori_loop(0, n_kt, body, jnp.zeros((1, tq), F32))

    m_ref[...] = jnp.full(m_ref.shape, -jnp.inf, F32)
    ls_ref[...] = jnp.zeros(ls_ref.shape, F32)
    ot_ref[...] = jnp.zeros(ot_ref.shape, F32)
    n_sub = tk // SCORE_ROWS
    ones_rows = jnp.ones((PACK_ROWS, tk), BF16)

    def stage_logits(kt, hh):
        r0 = pl.multiple_of(kt * tk, tk)
        s_ref[hh] = lax.dot_general(k_ref[hh // 2, pl.ds(r0, tk), :], qs_ref[hh], NT_DIMS,
                                    preferred_element_type=F32)

    for hh in range(N_HEADS):
        stage_logits(0, hh)

    def kt_body(kt, c):
        r0 = pl.multiple_of(kt * tk, tk)
        kt_next = jnp.minimum(kt + 1, n_kt - 1)

        mx = [None] * N_HEADS
        for r in range(n_sub):
            rows = slice(r * SCORE_ROWS, (r + 1) * SCORE_ROWS)
            bias = st_ref[pl.ds(r0 + r * SCORE_ROWS, SCORE_ROWS), :]
            for hh in range(N_HEADS):
                t = s_ref[hh, rows, :] + bias
                t_ref[hh, rows, :] = t
                m8 = _rows8(jnp.maximum, t)
                mx[hh] = m8 if mx[hh] is None else jnp.maximum(mx[hh], m8)

        for hh in range(LOGITS_AHEAD):
            stage_logits(kt_next, hh)
        for hh in range(N_HEADS):
            feat = slice(hh * HEAD_DIM, (hh + 1) * HEAD_DIM)
            m_prev = m_ref[hh]
            m_new = jnp.maximum(m_prev, jnp.max(mx[hh], axis=0, keepdims=True))
            m_ref[hh] = m_new
            alpha = jnp.exp2(m_prev - m_new)
            for r in range(n_sub):
                rows = slice(r * SCORE_ROWS, (r + 1) * SCORE_ROWS)
                p_ref[hh, rows, :] = jnp.exp2(t_ref[hh, rows, :] - m_new).astype(BF16)
            v_ext = jnp.concatenate([vt_ref[kt, feat, :], ones_rows], axis=0)
            pv = jnp.dot(v_ext, p_ref[hh], preferred_element_type=F32)
            ls_ref[hh] = alpha * ls_ref[hh] + pv[HEAD_DIM:HEAD_DIM + 1, :]
            ot_ref[feat, :] = alpha * ot_ref[feat, :] + pv[0:HEAD_DIM, :]
            if hh + LOGITS_AHEAD < N_HEADS:
                stage_logits(kt_next, hh + LOGITS_AHEAD)
        return c

    lax.fori_loop(0, n_kt, kt_body, 0)
    for hh in range(N_HEADS):
        feat = slice(hh * HEAD_DIM, (hh + 1) * HEAD_DIM)
        ot_ref[feat, :] = ot_ref[feat, :] / ls_ref[hh]
    o_ref[...] = ot_ref[...].T.astype(BF16)


def _attn_call(q, k, vt, iq, ikk, iwt, *, batch, seq):
    n = batch * seq
    nq = seq // TQ
    n_vt = seq // TK
    qspec = pl.BlockSpec((N_PAIRS, TQ, LANES), lambda b, i: (0, b * nq + i, 0))
    return pl.pallas_call(
        _attn_kernel,
        grid=(batch, nq),
        in_specs=[qspec,
                  pl.BlockSpec((N_PAIRS, seq, LANES), lambda b, i: (0, b, 0)),
                  pl.BlockSpec((n_vt, ATTN_WIDTH, TK), lambda b, i: (b, 0, 0)),
                  qspec,
                  pl.BlockSpec((seq, LANES), lambda b, i: (b, 0)),
                  pl.BlockSpec((IW_ROWS, TQ), lambda b, i: (0, b * nq + i))],
        out_specs=pl.BlockSpec((TQ, ATTN_WIDTH), lambda b, i: (b * nq + i, 0)),
        out_shape=jax.ShapeDtypeStruct((n, ATTN_WIDTH), BF16),
        scratch_shapes=[pltpu.VMEM((N_HEADS, TQ, LANES), BF16),
                        pltpu.VMEM((IDX_HEADS * TQ, LANES), BF16),
                        pltpu.VMEM((seq, TQ), F32),
                        pltpu.VMEM((seq, TQ), I16), pltpu.VMEM((seq, TQ), I16),
                        pltpu.VMEM((TK, IDX_HEADS * TQ), F32),
                        pltpu.VMEM((N_HEADS, 1, TQ), F32), pltpu.VMEM((N_HEADS, 1, TQ), F32),
                        pltpu.VMEM((ATTN_WIDTH, TQ), F32),
                        pltpu.VMEM((N_HEADS, TK, TQ), F32),
                        pltpu.VMEM((N_HEADS, TK, TQ), F32),
                        pltpu.VMEM((N_HEADS, TK, TQ), BF16)],
        compiler_params=pltpu.CompilerParams(dimension_semantics=("arbitrary", "arbitrary")),
        name="attn",
    )(q, k, vt, iq, ikk, iwt)


def _post_kernel(x_ref, a_ref, b_ref, woa_ref, wob_ref, gmix_ref, gffn_ref, w1_ref, w2_ref, gout_ref, o_ref):
    mix = (jnp.dot(a_ref[...], woa_ref[...], preferred_element_type=F32)
           + jnp.dot(b_ref[...], wob_ref[...], preferred_element_type=F32))
    x1 = x_ref[...] + _rms(mix, gmix_ref[...])
    h = _rms(x1, gffn_ref[...]).astype(BF16)
    f = None
    for c in range(D_FF // FF_CHUNK):
        cols = slice(c * FF_CHUNK, (c + 1) * FF_CHUNK)
        f1 = jnp.square(jnp.maximum(jnp.dot(h, w1_ref[:, cols], preferred_element_type=F32), 0.0)).astype(BF16)
        part = jnp.dot(f1, w2_ref[cols, :], preferred_element_type=F32)
        f = part if f is None else f + part
    o_ref[...] = x1 + _rms(f, gout_ref[...])


def _post_call(x2, a, b, woa, wob, gmix, gffn, w1, w2, gout):
    n = x2.shape[0]
    tm = ROW_TILE
    row = lambda width: pl.BlockSpec((tm, width), lambda i: (i, 0))
    gspec = _const_spec((1, D_MODEL))
    return pl.pallas_call(
        _post_kernel,
        grid=(n // tm,),
        in_specs=[row(D_MODEL), row(POOL_WIDTH), row(ATTN_WIDTH), _const_spec(woa.shape), _const_spec(wob.shape),
                  gspec, gspec, _const_spec(w1.shape), _const_spec(w2.shape), gspec],
        out_specs=row(D_MODEL),
        out_shape=jax.ShapeDtypeStruct((n, D_MODEL), F32),
        compiler_params=pltpu.CompilerParams(dimension_semantics=("arbitrary",),
                                             vmem_limit_bytes=56 * 1024 * 1024),
        name="post",
    )(x2, a, b, woa, wob, gmix, gffn, w1, w2, gout)


def _rope_tables(positions):
    inv = ROPE_THETA ** (-jnp.arange(ROPE_HALF, dtype=F32) / ROPE_HALF)
    ang = positions.astype(F32).reshape(-1, 1) * inv
    cos, sin = jnp.cos(ang), jnp.sin(ang)
    rest = HEAD_DIM - 2 * ROPE_HALF
    cos64 = jnp.concatenate([cos, cos, jnp.ones((cos.shape[0], rest), F32)], axis=-1)
    sin64 = jnp.concatenate([-sin, sin, jnp.zeros((sin.shape[0], rest), F32)], axis=-1)
    return jnp.tile(cos64, (1, LANES // HEAD_DIM)), jnp.tile(sin64, (1, LANES // HEAD_DIM))


def kernel(x, positions, g_pre_mix, w_in, w_pool, pool_scale, w_out, g_post_mix, g_pre_ffn, w_ff1, w_ff2,
           g_post_ffn):
    batch, seq, _ = x.shape
    depth = w_in.shape[0]
    assert seq % ROW_TILE == 0 and seq % TQ == 0 and TK % TQ == 0 and seq // 4 >= TOPK
    cos_t, sin_t = _rope_tables(positions)
    x2 = x.reshape(batch * seq, D_MODEL)
    c_v = POOL_WIDTH + 2 * ATTN_WIDTH
    c_iq = c_v + ATTN_WIDTH
    c_ik = c_iq + IDX_HEADS * IDX_DIM
    c_iw = c_ik + IDX_DIM
    for l in range(depth):
        wl = w_in[l]
        w_ik = wl[:, c_ik:c_iw]
        w_main = jnp.concatenate([wl[:, :c_v], wl[:, c_iq:c_ik], w_ik, w_ik], axis=1).astype(BF16)
        wvt = wl[:, c_v:c_iq].T.astype(BF16)
        wiwt = jnp.pad(wl[:, c_iw:].T, ((0, IW_ROWS - IDX_HEADS), (0, 0))).astype(BF16)
        a, q, k, vt, iq, ikk, iwt = _proj_call(
            x2, g_pre_mix[l][None], w_main, wvt, wiwt, cos_t, sin_t, w_pool[l].astype(BF16),
            pool_scale[l][None], seq=seq)
        b = _attn_call(q, k, vt, iq, ikk, iwt, batch=batch, seq=seq)
        wo = w_out[l].astype(BF16)
        x2 = _post_call(x2, a, b, wo[:POOL_WIDTH], wo[POOL_WIDTH:], g_post_mix[l][None], g_pre_ffn[l][None],
                        w_ff1[l].astype(BF16), w_ff2[l].astype(BF16), g_post_ffn[l][None])
    return x2.reshape(batch, seq, D_MODEL)
```

```python
import functools

import jax
import jax.numpy as jnp
from jax import lax
from jax.experimental import pallas as pl
from jax.experimental.pallas import tpu as pltpu

F32 = jnp.float32
BF16 = jnp.bfloat16
I32 = jnp.int32
I16 = jnp.int16

D_MODEL = 1024
CHUNK_SHIFT = 6
POOL_WIDTH = 512
POOL_WINDOWS = (2, 4, 8, 16)
POOL_GROUP = 128
POOL_HALO = 16
ATTN_WIDTH = 512
HEAD_DIM = 64
N_HEADS = 8
N_PAIRS = N_HEADS // 2
ROPE_HALF = 8
ROPE_THETA = 500000.0
IDX_HEADS = 8
IDX_DIM = 64
TOPK = 256
D_FF = 4096
EPS = 1e-6
LANES = 128
IW_ROWS = 16

ROW_TILE = 512
TQ = 256
TK = ROW_TILE
SCORE_ROWS = 64
FF_CHUNK = 1024
MASK_NEG = -1e30
LOG2_E = 1.4426950408889634
LOGITS_AHEAD = 2
PACK_ROWS = 16
COUNT_CHAINS = 4
I16_MIN, I16_MAX = -32768, 32767
F32_LOWEST = -3.4028234663852886e38

NT_DIMS = (((1,), (1,)), ((), ()))


def _rms(x, g):
    return x * lax.rsqrt(jnp.mean(x * x, axis=-1, keepdims=True) + EPS) * g


def _tree(fn, xs):
    xs = list(xs)
    while len(xs) > 1:
        nxt = [fn(xs[i], xs[i + 1]) for i in range(0, len(xs) - 1, 2)]
        if len(xs) % 2:
            nxt.append(xs[-1])
        xs = nxt
    return xs[0]


def _rows8(fn, x):
    return _tree(fn, [x[i:i + 8] for i in range(0, x.shape[0], 8)])


def _proj_kernel(x_ref, g_ref, w_ref, wvt_ref, wiwt_ref, cos_ref, sin_ref, wpool_ref, pscale_ref,
                 a_ref, q_ref, k_ref, vt_ref, iq_ref, ikk_ref, iwt_ref, ubuf_ref, *, tiles_per_seq):
    tm = x_ref.shape[0]
    t_in_seq = pl.program_id(0) % tiles_per_seq
    h = _rms(x_ref[...], g_ref[...]).astype(BF16)

    u = jnp.dot(h, w_ref[:, 0:POOL_WIDTH], preferred_element_type=F32)

    @pl.when(t_in_seq == 0)
    def _():
        ubuf_ref[0:POOL_HALO, :] = jnp.zeros((POOL_HALO, POOL_WIDTH), F32)

    @pl.when(t_in_seq != 0)
    def _():
        ubuf_ref[0:POOL_HALO, :] = ubuf_ref[tm:tm + POOL_HALO, :]

    ubuf_ref[POOL_HALO:POOL_HALO + tm, :] = u
    pos = t_in_seq * tm + lax.broadcasted_iota(I32, (tm, 1), 0)

    def pool_group(g):
        w = POOL_WINDOWS[g]
        c0 = g * POOL_GROUP
        ssum = ubuf_ref[:, c0:c0 + POOL_GROUP]
        span = 1
        while span < w:
            ssum = ssum + pltpu.roll(ssum, span, 0)
            span *= 2
        ug = u[:, c0:c0 + POOL_GROUP]
        cnt = jnp.minimum(pos + 1, w).astype(F32)
        d = (ssum[POOL_HALO:, :] / cnt - ug).astype(BF16)
        y = jnp.dot(d, wpool_ref[g], preferred_element_type=F32) * pscale_ref[:, c0:c0 + POOL_GROUP]
        a_ref[:, c0:c0 + POOL_GROUP] = y.astype(BF16)

    cos_t = cos_ref[...]
    sin_t = sin_ref[...]
    first_half = (lax.broadcasted_iota(I32, (tm, LANES), 1) % HEAD_DIM) < ROPE_HALF

    def rope(z):
        partner = jnp.where(first_half, pltpu.roll(z, LANES - ROPE_HALF, 1), pltpu.roll(z, ROPE_HALF, 1))
        return z * cos_t + partner * sin_t

    def head_group(col0, out_ref, scale):
        z = jnp.dot(h, w_ref[:, col0:col0 + ATTN_WIDTH], preferred_element_type=F32)
        for j in range(N_PAIRS):
            r = rope(z[:, j * LANES:(j + 1) * LANES])
            if scale != 1.0:
                r = r * scale
            out_ref[j] = r.astype(BF16)

    head_group(POOL_WIDTH, q_ref, HEAD_DIM ** -0.5 * LOG2_E)
    pool_group(0)
    head_group(POOL_WIDTH + ATTN_WIDTH, k_ref, 1.0)
    pool_group(1)
    head_group(POOL_WIDTH + 2 * ATTN_WIDTH, iq_ref, IDX_DIM ** -0.5)
    pool_group(2)
    col_ik = POOL_WIDTH + 3 * ATTN_WIDTH
    ikk = jnp.dot(h, w_ref[:, col_ik:col_ik + LANES], preferred_element_type=F32)
    ikk_ref[...] = rope(ikk).astype(BF16)

    vt_ref[0] = lax.dot_general(wvt_ref[...], h, NT_DIMS, preferred_element_type=F32).astype(BF16)
    iwt_ref[...] = lax.dot_general(wiwt_ref[...], h, NT_DIMS, preferred_element_type=F32) * (IDX_HEADS ** -0.5)
    pool_group(3)


def _const_spec(shape):
    return pl.BlockSpec(shape, lambda *_: (0,) * len(shape))


def _proj_call(x2, g, w_main, wvt, wiwt, cos_t, sin_t, wpool, pscale, *, seq):
    n = x2.shape[0]
    tm = ROW_TILE
    tps = seq // tm
    row = lambda width: pl.BlockSpec((tm, width), lambda i: (i, 0))
    pair = pl.BlockSpec((N_PAIRS, tm, LANES), lambda i: (0, i, 0))
    pair_shape = jax.ShapeDtypeStruct((N_PAIRS, n, LANES), BF16)
    return pl.pallas_call(
        functools.partial(_proj_kernel, tiles_per_seq=tps),
        grid=(n // tm,),
        in_specs=[row(D_MODEL), _const_spec((1, D_MODEL)), _const_spec(w_main.shape), _const_spec(wvt.shape),
                  _const_spec(wiwt.shape), row(LANES), row(LANES), _const_spec(wpool.shape),
                  _const_spec((1, POOL_WIDTH))],
        out_specs=[row(POOL_WIDTH), pair, pair,
                   pl.BlockSpec((1, ATTN_WIDTH, tm), lambda i: (i, 0, 0)),
                   pair, row(LANES), pl.BlockSpec((IW_ROWS, tm), lambda i: (0, i))],
        out_shape=[jax.ShapeDtypeStruct((n, POOL_WIDTH), BF16), pair_shape, pair_shape,
                   jax.ShapeDtypeStruct((n // tm, ATTN_WIDTH, tm), BF16),
                   pair_shape, jax.ShapeDtypeStruct((n, LANES), BF16),
                   jax.ShapeDtypeStruct((IW_ROWS, n), F32)],
        scratch_shapes=[pltpu.VMEM((POOL_HALO + tm, POOL_WIDTH), F32)],
        compiler_params=pltpu.CompilerParams(dimension_semantics=("arbitrary",)),
        name="proj",
    )(x2, g, w_main, wvt, wiwt, cos_t, sin_t, wpool, pscale)


def _key_of(x):
    b = lax.bitcast_convert_type(x, I32)
    sign = b >> 31
    return (b ^ (sign & 0x7FFFFFFF)) - sign


def _val_of(key):
    k = key + (key >> 31)
    return lax.bitcast_convert_type(k ^ ((k >> 31) & 0x7FFFFFFF), F32)


def _attn_kernel(q_ref, k_ref, vt_ref, iq_ref, ikk_ref, iwt_ref, o_ref,
                 qs_ref, iqs_ref, st_ref, khi_ref, klo_ref, lg_ref, m_ref, ls_ref, ot_ref,
                 s_ref, t_ref, p_ref):
    tq = q_ref.shape[1]
    tk = vt_ref.shape[2]
    q0 = pl.program_id(1) * tq
    n_kt = (q0 + tq + tk - 1) // tk
    q_chunk = (q0 + lax.broadcasted_iota(I32, (1, tq), 1)) >> CHUNK_SHIFT

    low = lax.broadcasted_iota(I32, (tq, LANES), 1) < HEAD_DIM
    for j in range(N_PAIRS):
        qp = q_ref[j].astype(F32)
        qs_ref[2 * j] = jnp.where(low, qp, 0.0).astype(BF16)
        qs_ref[2 * j + 1] = jnp.where(low, 0.0, qp).astype(BF16)
        ip = iq_ref[j].astype(F32)
        iqs_ref[2 * j * tq:(2 * j + 1) * tq, :] = jnp.where(low, ip, 0.0).astype(BF16)
        iqs_ref[(2 * j + 1) * tq:(2 * j + 2) * tq, :] = jnp.where(low, 0.0, ip).astype(BF16)

    def score_tile(kt, masked):
        r0 = pl.multiple_of(kt * tk, tk)
        lg_ref[...] = lax.dot_general(ikk_ref[pl.ds(r0, tk), :], iqs_ref[...], NT_DIMS,
                                      preferred_element_type=F32)
        for r in range(tk // SCORE_ROWS):
            rows = slice(r * SCORE_ROWS, (r + 1) * SCORE_ROWS)
            sc = None
            for hh in range(IDX_HEADS):
                c = iwt_ref[hh:hh + 1, :] * jnp.maximum(lg_ref[rows, hh * tq:(hh + 1) * tq], 0.0)
                sc = c if sc is None else sc + c
            if masked:
                k_idx = r0 + r * SCORE_ROWS + lax.broadcasted_iota(I32, (SCORE_ROWS, tq), 0)
                allowed = (k_idx >> CHUNK_SHIFT) <= q_chunk
                sc = jnp.where(allowed, sc, -jnp.inf)
            out_rows = pl.ds(r0 + r * SCORE_ROWS, SCORE_ROWS)
            st_ref[out_rows, :] = sc
            key = _key_of(sc)
            khi_ref[out_rows, :] = (key >> 16).astype(I16)
            klo_ref[out_rows, :] = (key ^ 0x8000).astype(I16)

    def score_body(kt, c):
        score_tile(kt, False)
        return c

    lax.fori_loop(0, n_kt - 1, score_body, 0)
    score_tile(n_kt - 1, True)

    def count16(src_ref, cand):
        c16 = cand.astype(I16)

        def body(kt, accs):
            blk = src_ref[pl.ds(pl.multiple_of(kt * tk, tk), tk), :]
            ind = jnp.where(blk >= c16, jnp.int16(1), jnp.int16(0))
            parts = [ind[i:i + PACK_ROWS] for i in range(0, tk, PACK_ROWS)]
            return tuple(a + _tree(jnp.add, parts[j::len(accs)]) for j, a in enumerate(accs))

        zero = jnp.zeros((PACK_ROWS, tq), I16)
        accs = lax.fori_loop(0, n_kt, body, (zero,) * COUNT_CHAINS)
        return jnp.sum(_tree(jnp.add, list(accs)).astype(I32), axis=0, keepdims=True)

    n_allowed = (q_chunk + 1) << CHUNK_SHIFT
    searched = n_allowed > TOPK
    n_stored = jnp.full((1, tq), n_kt * tk, I32)

    def radix_step(src_ref, need, bit, state):
        t, c_t, c_refused = state
        cand = t + jnp.left_shift(jnp.int32(1), bit)
        c = count16(src_ref, cand)
        ok = c >= need
        return jnp.where(ok, cand, t), jnp.where(ok, c, c_t), jnp.where(ok, c_refused, c)

    start = (jnp.full((1, tq), I16_MIN, I32), n_stored, jnp.zeros((1, tq), I32))
    t_hi, _, c_above = lax.fori_loop(0, 16, lambda i, st: radix_step(khi_ref, TOPK, 15 - i, st), start)
    t_hi16 = t_hi.astype(I16)

    def keep_bucket(kt, c):
        rows = pl.ds(pl.multiple_of(kt * tk, tk), tk)
        klo_ref[rows, :] = jnp.where(khi_ref[rows, :] == t_hi16, klo_ref[rows, :], jnp.int16(I16_MIN))
        return c

    lax.fori_loop(0, n_kt, keep_bucket, 0)

    t_lo, c_bucket, _ = lax.fori_loop(
        0, 16, lambda i, st: radix_step(klo_ref, TOPK - c_above, 15 - i, st), start)
    thr_key = (t_hi << 16) | (t_lo + 0x8000)
    thr = jnp.where(searched, _val_of(thr_key), F32_LOWEST)
    c_lo = jnp.where(searched, c_above + c_bucket, n_allowed).astype(F32)

    has_tie = jnp.max(jnp.where(c_lo > TOPK, 1.0, 0.0)) > 0.5

    @pl.when(jnp.logical_not(has_tie))
    def _():
        def body(kt, c):
            rows = pl.ds(pl.multiple_of(kt * tk, tk), tk)
            st_ref[rows, :] = jnp.where(st_ref[rows, :] >= thr, 0.0, MASK_NEG)
            return c
        lax.fori_loop(0, n_kt, body, 0)

    @pl.when(has_tie)
    def _():
        def count_above(kt, acc):
            blk = st_ref[pl.ds(pl.multiple_of(kt * tk, tk), tk), :]
            return acc + _rows8(jnp.add, jnp.where(blk > thr, 1.0, 0.0))
        above = lax.fori_loop(0, n_kt, count_above, jnp.zeros((8, tq), F32))
        budget = TOPK - jnp.sum(above, axis=0, keepdims=True)
        ri = lax.broadcasted_iota(I32, (tk, tk), 0)
        ci = lax.broadcasted_iota(I32, (tk, tk), 1)
        prefix = jnp.where(ci <= ri, 1.0, 0.0).astype(BF16)

        def body(kt, seen):
            rows = pl.ds(pl.multiple_of(kt * tk, tk), tk)
            blk = st_ref[rows, :]
            eq = blk == thr
            rank = jnp.dot(prefix, jnp.where(eq, 1.0, 0.0).astype(BF16), preferred_element_type=F32) + seen
            keep = (blk > thr) | (eq & (rank <= budget))
            st_ref[rows, :] = jnp.where(keep, 0.0, MASK_NEG)
            return rank[tk - 1:tk, :]
        lax.fori_loop(0, n_kt, body, jnp.zeros((1, tq), F32))

    m_ref[...] = jnp.full(m_ref.shape, -jnp.inf, F32)
    ls_ref[...] = jnp.zeros(ls_ref.shape, F32)
    ot_ref[...] = jnp.zeros(ot_ref.shape, F32)
    n_sub = tk // SCORE_ROWS
    ones_rows = jnp.ones((PACK_ROWS, tk), BF16)

    def stage_logits(kt, hh):
        r0 = pl.multiple_of(kt * tk, tk)
        s_ref[hh] = lax.dot_general(k_ref[hh // 2, pl.ds(r0, tk), :], qs_ref[hh], NT_DIMS,
                                    preferred_element_type=F32)

    for hh in range(N_HEADS):
        stage_logits(0, hh)

    def kt_body(kt, c):
        r0 = pl.multiple_of(kt * tk, tk)
        kt_next = jnp.minimum(kt + 1, n_kt - 1)

        mx = [None] * N_HEADS
        for r in range(n_sub):
            rows = slice(r * SCORE_ROWS, (r + 1) * SCORE_ROWS)
            bias = st_ref[pl.ds(r0 + r * SCORE_ROWS, SCORE_ROWS), :]
            for hh in range(N_HEADS):
                t = s_ref[hh, rows, :] + bias
                t_ref[hh, rows, :] = t
                m8 = _rows8(jnp.maximum, t)
                mx[hh] = m8 if mx[hh] is None else jnp.maximum(mx[hh], m8)

        for hh in range(LOGITS_AHEAD):
            stage_logits(kt_next, hh)
        for hh in range(N_HEADS):
            feat = slice(hh * HEAD_DIM, (hh + 1) * HEAD_DIM)
            m_prev = m_ref[hh]
            m_new = jnp.maximum(m_prev, jnp.max(mx[hh], axis=0, keepdims=True))
            m_ref[hh] = m_new
            alpha = jnp.exp2(m_prev - m_new)
            for r in range(n_sub):
                rows = slice(r * SCORE_ROWS, (r + 1) * SCORE_ROWS)
                p_ref[hh, rows, :] = jnp.exp2(t_ref[hh, rows, :] - m_new).astype(BF16)
            v_ext = jnp.concatenate([vt_ref[kt, feat, :], ones_rows], axis=0)
            pv = jnp.dot(v_ext, p_ref[hh], preferred_element_type=F32)
            ls_ref[hh] = alpha * ls_ref[hh] + pv[HEAD_DIM:HEAD_DIM + 1, :]
            ot_ref[feat, :] = alpha * ot_ref[feat, :] + pv[0:HEAD_DIM, :]
            if hh + LOGITS_AHEAD < N_HEADS:
                stage_logits(kt_next, hh + LOGITS_AHEAD)
        return c

    lax.fori_loop(0, n_kt, kt_body, 0)
    for hh in range(N_HEADS):
        feat = slice(hh * HEAD_DIM, (hh + 1) * HEAD_DIM)
        ot_ref[feat, :] = ot_ref[feat, :] / ls_ref[hh]
    o_ref[...] = ot_ref[...].T.astype(BF16)


def _attn_call(q, k, vt, iq, ikk, iwt, *, batch, seq):
    n = batch * seq
    nq = seq // TQ
    n_vt = seq // TK
    qspec = pl.BlockSpec((N_PAIRS, TQ, LANES), lambda b, i: (0, b * nq + i, 0))
    return pl.pallas_call(
        _attn_kernel,
        grid=(batch, nq),
        in_specs=[qspec,
                  pl.BlockSpec((N_PAIRS, seq, LANES), lambda b, i: (0, b, 0)),
                  pl.BlockSpec((n_vt, ATTN_WIDTH, TK), lambda b, i: (b, 0, 0)),
                  qspec,
                  pl.BlockSpec((seq, LANES), lambda b, i: (b, 0)),
                  pl.BlockSpec((IW_ROWS, TQ), lambda b, i: (0, b * nq + i))],
        out_specs=pl.BlockSpec((TQ, ATTN_WIDTH), lambda b, i: (b * nq + i, 0)),
        out_shape=jax.ShapeDtypeStruct((n, ATTN_WIDTH), BF16),
        scratch_shapes=[pltpu.VMEM((N_HEADS, TQ, LANES), BF16),
                        pltpu.VMEM((IDX_HEADS * TQ, LANES), BF16),
                        pltpu.VMEM((seq, TQ), F32),
                        pltpu.VMEM((seq, TQ), I16), pltpu.VMEM((seq, TQ), I16),
                        pltpu.VMEM((TK, IDX_HEADS * TQ), F32),
                        pltpu.VMEM((N_HEADS, 1, TQ), F32), pltpu.VMEM((N_HEADS, 1, TQ), F32),
                        pltpu.VMEM((ATTN_WIDTH, TQ), F32),
                        pltpu.VMEM((N_HEADS, TK, TQ), F32),
                        pltpu.VMEM((N_HEADS, TK, TQ), F32),
                        pltpu.VMEM((N_HEADS, TK, TQ), BF16)],
        compiler_params=pltpu.CompilerParams(dimension_semantics=("arbitrary", "arbitrary")),
        name="attn",
    )(q, k, vt, iq, ikk, iwt)


def _post_kernel(x_ref, a_ref, b_ref, woa_ref, wob_ref, gmix_ref, gffn_ref, w1_ref, w2_ref, gout_ref, o_ref):
    mix = (jnp.dot(a_ref[...], woa_ref[...], preferred_element_type=F32)
           + jnp.dot(b_ref[...], wob_ref[...], preferred_element_type=F32))
    x1 = x_ref[...] + _rms(mix, gmix_ref[...])
    h = _rms(x1, gffn_ref[...]).astype(BF16)
    f = None
    for c in range(D_FF // FF_CHUNK):
        cols = slice(c * FF_CHUNK, (c + 1) * FF_CHUNK)
        f1 = jnp.square(jnp.maximum(jnp.dot(h, w1_ref[:, cols], preferred_element_type=F32), 0.0)).astype(BF16)
        part = jnp.dot(f1, w2_ref[cols, :], preferred_element_type=F32)
        f = part if f is None else f + part
    o_ref[...] = x1 + _rms(f, gout_ref[...])


def _post_call(x2, a, b, woa, wob, gmix, gffn, w1, w2, gout):
    n = x2.shape[0]
    tm = ROW_TILE
    row = lambda width: pl.BlockSpec((tm, width), lambda i: (i, 0))
    gspec = _const_spec((1, D_MODEL))
    return pl.pallas_call(
        _post_kernel,
        grid=(n // tm,),
        in_specs=[row(D_MODEL), row(POOL_WIDTH), row(ATTN_WIDTH), _const_spec(woa.shape), _const_spec(wob.shape),
                  gspec, gspec, _const_spec(w1.shape), _const_spec(w2.shape), gspec],
        out_specs=row(D_MODEL),
        out_shape=jax.ShapeDtypeStruct((n, D_MODEL), F32),
        compiler_params=pltpu.CompilerParams(dimension_semantics=("arbitrary",),
                                             vmem_limit_bytes=56 * 1024 * 1024),
        name="post",
    )(x2, a, b, woa, wob, gmix, gffn, w1, w2, gout)


def _rope_tables(positions):
    inv = ROPE_THETA ** (-jnp.arange(ROPE_HALF, dtype=F32) / ROPE_HALF)
    ang = positions.astype(F32).reshape(-1, 1) * inv
    cos, sin = jnp.cos(ang), jnp.sin(ang)
    rest = HEAD_DIM - 2 * ROPE_HALF
    cos64 = jnp.concatenate([cos, cos, jnp.ones((cos.shape[0], rest), F32)], axis=-1)
    sin64 = jnp.concatenate([-sin, sin, jnp.zeros((sin.shape[0], rest), F32)], axis=-1)
    return jnp.tile(cos64, (1, LANES // HEAD_DIM)), jnp.tile(sin64, (1, LANES // HEAD_DIM))


def kernel(x, positions, g_pre_mix, w_in, w_pool, pool_scale, w_out, g_post_mix, g_pre_ffn, w_ff1, w_ff2,
           g_post_ffn):
    batch, seq, _ = x.shape
    depth = w_in.shape[0]
    assert seq % ROW_TILE == 0 and seq % TQ == 0 and TK % TQ == 0 and seq // 4 >= TOPK
    cos_t, sin_t = _rope_tables(positions)
    x2 = x.reshape(batch * seq, D_MODEL)
    c_v = POOL_WIDTH + 2 * ATTN_WIDTH
    c_iq = c_v + ATTN_WIDTH
    c_ik = c_iq + IDX_HEADS * IDX_DIM
    c_iw = c_ik + IDX_DIM
    for l in range(depth):
        wl = w_in[l]
        w_ik = wl[:, c_ik:c_iw]
        w_main = jnp.concatenate([wl[:, :c_v], wl[:, c_iq:c_ik], w_ik, w_ik], axis=1).astype(BF16)
        wvt = wl[:, c_v:c_iq].T.astype(BF16)
        wiwt = jnp.pad(wl[:, c_iw:].T, ((0, IW_ROWS - IDX_HEADS), (0, 0))).astype(BF16)
        a, q, k, vt, iq, ikk, iwt = _proj_call(
            x2, g_pre_mix[l][None], w_main, wvt, wiwt, cos_t, sin_t, w_pool[l].astype(BF16),
            pool_scale[l][None], seq=seq)
        b = _attn_call(q, k, vt, iq, ikk, iwt, batch=batch, seq=seq)
        wo = w_out[l].astype(BF16)
        x2 = _post_call(x2, a, b, wo[:POOL_WIDTH], wo[POOL_WIDTH:], g_post_mix[l][None], g_pre_ffn[l][None],
                        w_ff1[l].astype(BF16), w_ff2[l].astype(BF16), g_post_ffn[l][None])
    return x2.reshape(batch, seq, D_MODEL)
```

```python
import functools

import jax
import jax.numpy as jnp
from jax import lax
from jax.experimental import pallas as pl
from jax.experimental.pallas import tpu as pltpu

F32 = jnp.float32
BF16 = jnp.bfloat16
I32 = jnp.int32
I16 = jnp.int16

D_MODEL = 1024
CHUNK_SHIFT = 6
POOL_WIDTH = 512
POOL_WINDOWS = (2, 4, 8, 16)
POOL_GROUP = 128
POOL_HALO = 16
ATTN_WIDTH = 512
HEAD_DIM = 64
N_HEADS = 8
N_PAIRS = N_HEADS // 2
ROPE_HALF = 8
ROPE_THETA = 500000.0
IDX_HEADS = 8
IDX_DIM = 64
TOPK = 256
D_FF = 4096
EPS = 1e-6
LANES = 128
IW_ROWS = 16

ROW_TILE = 512
TQ = 256
TK = ROW_TILE
SCORE_ROWS = 64
FF_CHUNK = 1024
MASK_NEG = -1e30
LOG2_E = 1.4426950408889634
LOGITS_AHEAD = 2
PACK_ROWS = 16
COUNT_CHAINS = 4
I16_MIN, I16_MAX = -32768, 32767
F32_LOWEST = -3.4028234663852886e38

NT_DIMS = (((1,), (1,)), ((), ()))


def _rms(x, g):
    return x * lax.rsqrt(jnp.mean(x * x, axis=-1, keepdims=True) + EPS) * g


def _tree(fn, xs):
    xs = list(xs)
    while len(xs) > 1:
        nxt = [fn(xs[i], xs[i + 1]) for i in range(0, len(xs) - 1, 2)]
        if len(xs) % 2:
            nxt.append(xs[-1])
        xs = nxt
    return xs[0]


def _rows8(fn, x):
    return _tree(fn, [x[i:i + 8] for i in range(0, x.shape[0], 8)])


def _proj_kernel(x_ref, g_ref, w_ref, wvt_ref, wiwt_ref, cos_ref, sin_ref, wpool_ref, pscale_ref,
                 a_ref, q_ref, k_ref, vt_ref, iq_ref, ikk_ref, iwt_ref, ubuf_ref, *, tiles_per_seq):
    tm = x_ref.shape[0]
    t_in_seq = pl.program_id(0) % tiles_per_seq
    h = _rms(x_ref[...], g_ref[...]).astype(BF16)

    u = jnp.dot(h, w_ref[:, 0:POOL_WIDTH], preferred_element_type=F32)

    @pl.when(t_in_seq == 0)
    def _():
        ubuf_ref[0:POOL_HALO, :] = jnp.zeros((POOL_HALO, POOL_WIDTH), F32)

    @pl.when(t_in_seq != 0)
    def _():
        ubuf_ref[0:POOL_HALO, :] = ubuf_ref[tm:tm + POOL_HALO, :]

    ubuf_ref[POOL_HALO:POOL_HALO + tm, :] = u
    pos = t_in_seq * tm + lax.broadcasted_iota(I32, (tm, 1), 0)

    def pool_group(g):
        w = POOL_WINDOWS[g]
        c0 = g * POOL_GROUP
        ssum = ubuf_ref[:, c0:c0 + POOL_GROUP]
        span = 1
        while span < w:
            ssum = ssum + pltpu.roll(ssum, span, 0)
            span *= 2
        ug = u[:, c0:c0 + POOL_GROUP]
        cnt = jnp.minimum(pos + 1, w).astype(F32)
        d = (ssum[POOL_HALO:, :] / cnt - ug).astype(BF16)
        y = jnp.dot(d, wpool_ref[g], preferred_element_type=F32) * pscale_ref[:, c0:c0 + POOL_GROUP]
        a_ref[:, c0:c0 + POOL_GROUP] = y.astype(BF16)

    cos_t = cos_ref[...]
    sin_t = sin_ref[...]
    first_half = (lax.broadcasted_iota(I32, (tm, LANES), 1) % HEAD_DIM) < ROPE_HALF

    def rope(z):
        partner = jnp.where(first_half, pltpu.roll(z, LANES - ROPE_HALF, 1), pltpu.roll(z, ROPE_HALF, 1))
        return z * cos_t + partner * sin_t

    def head_group(col0, out_ref, scale):
        z = jnp.dot(h, w_ref[:, col0:col0 + ATTN_WIDTH], preferred_element_type=F32)
        for j in range(N_PAIRS):
            r = rope(z[:, j * LANES:(j + 1) * LANES])
            if scale != 1.0:
                r = r * scale
            out_ref[j] = r.astype(BF16)

    head_group(POOL_WIDTH, q_ref, HEAD_DIM ** -0.5 * LOG2_E)
    pool_group(0)
    head_group(POOL_WIDTH + ATTN_WIDTH, k_ref, 1.0)
    pool_group(1)
    head_group(POOL_WIDTH + 2 * ATTN_WIDTH, iq_ref, IDX_DIM ** -0.5)
    pool_group(2)
    col_ik = POOL_WIDTH + 3 * ATTN_WIDTH
    ikk = jnp.dot(h, w_ref[:, col_ik:col_ik + LANES], preferred_element_type=F32)
    ikk_ref[...] = rope(ikk).astype(BF16)

    vt_ref[0] = lax.dot_general(wvt_ref[...], h, NT_DIMS, preferred_element_type=F32).astype(BF16)
    iwt_ref[...] = lax.dot_general(wiwt_ref[...], h, NT_DIMS, preferred_element_type=F32) * (IDX_HEADS ** -0.5)
    pool_group(3)


def _const_spec(shape):
    return pl.BlockSpec(shape, lambda *_: (0,) * len(shape))


def _proj_call(x2, g, w_main, wvt, wiwt, cos_t, sin_t, wpool, pscale, *, seq):
    n = x2.shape[0]
    tm = ROW_TILE
    tps = seq // tm
    row = lambda width: pl.BlockSpec((tm, width), lambda i: (i, 0))
    pair = pl.BlockSpec((N_PAIRS, tm, LANES), lambda i: (0, i, 0))
    pair_shape = jax.ShapeDtypeStruct((N_PAIRS, n, LANES), BF16)
    return pl.pallas_call(
        functools.partial(_proj_kernel, tiles_per_seq=tps),
        grid=(n // tm,),
        in_specs=[row(D_MODEL), _const_spec((1, D_MODEL)), _const_spec(w_main.shape), _const_spec(wvt.shape),
                  _const_spec(wiwt.shape), row(LANES), row(LANES), _const_spec(wpool.shape),
                  _const_spec((1, POOL_WIDTH))],
        out_specs=[row(POOL_WIDTH), pair, pair,
                   pl.BlockSpec((1, ATTN_WIDTH, tm), lambda i: (i, 0, 0)),
                   pair, row(LANES), pl.BlockSpec((IW_ROWS, tm), lambda i: (0, i))],
        out_shape=[jax.ShapeDtypeStruct((n, POOL_WIDTH), BF16), pair_shape, pair_shape,
                   jax.ShapeDtypeStruct((n // tm, ATTN_WIDTH, tm), BF16),
                   pair_shape, jax.ShapeDtypeStruct((n, LANES), BF16),
                   jax.ShapeDtypeStruct((IW_ROWS, n), F32)],
        scratch_shapes=[pltpu.VMEM((POOL_HALO + tm, POOL_WIDTH), F32)],
        compiler_params=pltpu.CompilerParams(dimension_semantics=("arbitrary",)),
        name="proj",
    )(x2, g, w_main, wvt, wiwt, cos_t, sin_t, wpool, pscale)


def _key_of(x):
    b = lax.bitcast_convert_type(x, I32)
    sign = b >> 31
    return (b ^ (sign & 0x7FFFFFFF)) - sign


def _val_of(key):
    k = key + (key >> 31)
    return lax.bitcast_convert_type(k ^ ((k >> 31) & 0x7FFFFFFF), F32)


def _attn_kernel(q_ref, k_ref, vt_ref, iq_ref, ikk_ref, iwt_ref, o_ref,
                 qs_ref, iqs_ref, st_ref, khi_ref, klo_ref, lg_ref, m_ref, ls_ref, ot_ref,
                 s_ref, t_ref, p_ref):
    tq = q_ref.shape[1]
    tk = vt_ref.shape[2]
    q0 = pl.program_id(1) * tq
    n_kt = (q0 + tq + tk - 1) // tk
    q_chunk = (q0 + lax.broadcasted_iota(I32, (1, tq), 1)) >> CHUNK_SHIFT

    low = lax.broadcasted_iota(I32, (tq, LANES), 1) < HEAD_DIM
    for j in range(N_PAIRS):
        qp = q_ref[j].astype(F32)
        qs_ref[2 * j] = jnp.where(low, qp, 0.0).astype(BF16)
        qs_ref[2 * j + 1] = jnp.where(low, 0.0, qp).astype(BF16)
        ip = iq_ref[j].astype(F32)
        iqs_ref[2 * j * tq:(2 * j + 1) * tq, :] = jnp.where(low, ip, 0.0).astype(BF16)
        iqs_ref[(2 * j + 1) * tq:(2 * j + 2) * tq, :] = jnp.where(low, 0.0, ip).astype(BF16)

    def stage_logits(kt, hh):
        r0 = pl.multiple_of(kt * tk, tk)
        s_ref[hh] = lax.dot_general(k_ref[hh // 2, pl.ds(r0, tk), :], qs_ref[hh], NT_DIMS,
                                    preferred_element_type=F32)

    def score_tile(kt, masked):
        r0 = pl.multiple_of(kt * tk, tk)
        lg_ref[...] = lax.dot_general(ikk_ref[pl.ds(r0, tk), :], iqs_ref[...], NT_DIMS,
                                      preferred_element_type=F32)
        if masked:
            for hh in range(N_HEADS):
                stage_logits(0, hh)
        for r in range(tk // SCORE_ROWS):
            rows = slice(r * SCORE_ROWS, (r + 1) * SCORE_ROWS)
            sc = None
            for hh in range(IDX_HEADS):
                c = iwt_ref[hh:hh + 1, :] * jnp.maximum(lg_ref[rows, hh * tq:(hh + 1) * tq], 0.0)
                sc = c if sc is None else sc + c
            if masked:
                allowed = ((r0 + r * SCORE_ROWS) >> CHUNK_SHIFT) <= q_chunk
                sc = jnp.where(allowed, sc, -jnp.inf)
            out_rows = pl.ds(r0 + r * SCORE_ROWS, SCORE_ROWS)
            st_ref[out_rows, :] = sc
            key = _key_of(sc)
            khi_ref[out_rows, :] = (key >> 16).astype(I16)
            klo_ref[out_rows, :] = (key ^ 0x8000).astype(I16)

    def score_body(kt, c):
        score_tile(kt, False)
        return c

    lax.fori_loop(0, n_kt - 1, score_body, 0)
    score_tile(n_kt - 1, True)

    def count16(src_ref, cand):
        c16 = cand.astype(I16)

        def body(kt, accs):
            blk = src_ref[pl.ds(pl.multiple_of(kt * tk, tk), tk), :]
            ind = jnp.where(blk >= c16, jnp.int16(1), jnp.int16(0))
            parts = [ind[i:i + PACK_ROWS] for i in range(0, tk, PACK_ROWS)]
            return tuple(a + _tree(jnp.add, parts[j::len(accs)]) for j, a in enumerate(accs))

        zero = jnp.zeros((PACK_ROWS, tq), I16)
        accs = lax.fori_loop(0, n_kt, body, (zero,) * COUNT_CHAINS)
        return jnp.sum(_tree(jnp.add, list(accs)).astype(I32), axis=0, keepdims=True)

    n_allowed = (q_chunk + 1) << CHUNK_SHIFT
    searched = n_allowed > TOPK
    n_stored = jnp.full((1, tq), n_kt * tk, I32)

    def radix_step(src_ref, need, bit, state):
        t, c_t, c_refused = state
        cand = t + jnp.left_shift(jnp.int32(1), bit)
        c = count16(src_ref, cand)
        ok = c >= need
        return jnp.where(ok, cand, t), jnp.where(ok, c, c_t), jnp.where(ok, c_refused, c)

    start = (jnp.full((1, tq), I16_MIN, I32), n_stored, jnp.zeros((1, tq), I32))
    t_hi, _, c_above = lax.fori_loop(0, 16, lambda i, st: radix_step(khi_ref, TOPK, 15 - i, st), start)
    t_hi16 = t_hi.astype(I16)

    def keep_bucket(kt, c):
        rows = pl.ds(pl.multiple_of(kt * tk, tk), tk)
        klo_ref[rows, :] = jnp.where(khi_ref[rows, :] == t_hi16, klo_ref[rows, :], jnp.int16(I16_MIN))
        return c

    lax.fori_loop(0, n_kt, keep_bucket, 0)

    t_lo, c_bucket, _ = lax.fori_loop(
        0, 16, lambda i, st: radix_step(klo_ref, TOPK - c_above, 15 - i, st), start)
    thr_key = (t_hi << 16) | (t_lo + 0x8000)
    thr = jnp.where(searched, _val_of(thr_key), F32_LOWEST)
    c_lo = jnp.where(searched, c_above + c_bucket, n_allowed).astype(F32)

    has_tie = jnp.max(jnp.where(c_lo > TOPK, 1.0, 0.0)) > 0.5

    @pl.when(jnp.logical_not(has_tie))
    def _():
        def body(kt, c):
            rows = pl.ds(pl.multiple_of(kt * tk, tk), tk)
            st_ref[rows, :] = jnp.where(st_ref[rows, :] >= thr, 0.0, MASK_NEG)
            return c
        lax.fori_loop(0, n_kt, body, 0)

    @pl.when(has_tie)
    def _():
        def count_above(kt, acc):
            blk = st_ref[pl.ds(pl.multiple_of(kt * tk, tk), tk), :]
            return acc + _rows8(jnp.add, jnp.where(blk > thr, 1.0, 0.0))
        above = lax.fori_loop(0, n_kt, count_above, jnp.zeros((8, tq), F32))
        budget = TOPK - jnp.sum(above, axis=0, keepdims=True)
        ri = lax.broadcasted_iota(I32, (tk, tk), 0)
        ci = lax.broadcasted_iota(I32, (tk, tk), 1)
        prefix = jnp.where(ci <= ri, 1.0, 0.0).astype(BF16)

        def body(kt, seen):
            rows = pl.ds(pl.multiple_of(kt * tk, tk), tk)
            blk = st_ref[rows, :]
            eq = blk == thr
            rank = jnp.dot(prefix, jnp.where(eq, 1.0, 0.0).astype(BF16), preferred_element_type=F32) + seen
            keep = (blk > thr) | (eq & (rank <= budget))
            st_ref[rows, :] = jnp.where(keep, 0.0, MASK_NEG)
            return rank[tk - 1:tk, :]
        lax.fori_loop(0, n_kt, body, jnp.zeros((1, tq), F32))

    m_ref[...] = jnp.full(m_ref.shape, -jnp.inf, F32)
    ls_ref[...] = jnp.zeros(ls_ref.shape, F32)
    ot_ref[...] = jnp.zeros(ot_ref.shape, F32)
    n_sub = tk // SCORE_ROWS
    ones_rows = jnp.ones((PACK_ROWS, tk), BF16)

    def kt_body(kt, c):
        r0 = pl.multiple_of(kt * tk, tk)
        kt_next = jnp.minimum(kt + 1, n_kt - 1)

        mx = [None] * N_HEADS
        for r in range(n_sub):
            rows = slice(r * SCORE_ROWS, (r + 1) * SCORE_ROWS)
            bias = st_ref[pl.ds(r0 + r * SCORE_ROWS, SCORE_ROWS), :]
            for hh in range(N_HEADS):
                t = s_ref[hh, rows, :] + bias
                t_ref[hh, rows, :] = t
                m8 = _rows8(jnp.maximum, t)
                mx[hh] = m8 if mx[hh] is None else jnp.maximum(mx[hh], m8)

        for hh in range(LOGITS_AHEAD):
            stage_logits(kt_next, hh)
        for hh in range(N_HEADS):
            feat = slice(hh * HEAD_DIM, (hh + 1) * HEAD_DIM)
            m_prev = m_ref[hh]
            m_new = jnp.maximum(m_prev, jnp.max(mx[hh], axis=0, keepdims=True))
            m_ref[hh] = m_new
            alpha = jnp.exp2(m_prev - m_new)
            for r in range(n_sub):
                rows = slice(r * SCORE_ROWS, (r + 1) * SCORE_ROWS)
                p_ref[hh, rows, :] = jnp.exp2(t_ref[hh, rows, :] - m_new).astype(BF16)
            v_ext = jnp.concatenate([vt_ref[kt, feat, :], ones_rows], axis=0)
            pv = jnp.dot(v_ext, p_ref[hh], preferred_element_type=F32)
            ls_ref[hh] = alpha * ls_ref[hh] + pv[HEAD_DIM:HEAD_DIM + 1, :]
            ot_ref[feat, :] = alpha * ot_ref[feat, :] + pv[0:HEAD_DIM, :]
            if hh + LOGITS_AHEAD < N_HEADS:
                stage_logits(kt_next, hh + LOGITS_AHEAD)
        return c

    lax.fori_loop(0, n_kt, kt_body, 0)
    for hh in range(N_HEADS):
        feat = slice(hh * HEAD_DIM, (hh + 1) * HEAD_DIM)
        ot_ref[feat, :] = ot_ref[feat, :] / ls_ref[hh]
    o_ref[...] = ot_ref[...].T.astype(BF16)


def _attn_call(q, k, vt, iq, ikk, iwt, *, batch, seq):
    n = batch * seq
    nq = seq // TQ
    n_vt = seq // TK
    qspec = pl.BlockSpec((N_PAIRS, TQ, LANES), lambda b, i: (0, b * nq + i, 0))
    return pl.pallas_call(
        _attn_kernel,
        grid=(batch, nq),
        in_specs=[qspec,
                  pl.BlockSpec((N_PAIRS, seq, LANES), lambda b, i: (0, b, 0)),
                  pl.BlockSpec((n_vt, ATTN_WIDTH, TK), lambda b, i: (b, 0, 0)),
                  qspec,
                  pl.BlockSpec((seq, LANES), lambda b, i: (b, 0)),
                  pl.BlockSpec((IW_ROWS, TQ), lambda b, i: (0, b * nq + i))],
        out_specs=pl.BlockSpec((TQ, ATTN_WIDTH), lambda b, i: (b * nq + i, 0)),
        out_shape=jax.ShapeDtypeStruct((n, ATTN_WIDTH), BF16),
        scratch_shapes=[pltpu.VMEM((N_HEADS, TQ, LANES), BF16),
                        pltpu.VMEM((IDX_HEADS * TQ, LANES), BF16),
                        pltpu.VMEM((seq, TQ), F32),
                        pltpu.VMEM((seq, TQ), I16), pltpu.VMEM((seq, TQ), I16),
                        pltpu.VMEM((TK, IDX_HEADS * TQ), F32),
                        pltpu.VMEM((N_HEADS, 1, TQ), F32), pltpu.VMEM((N_HEADS, 1, TQ), F32),
                        pltpu.VMEM((ATTN_WIDTH, TQ), F32),
                        pltpu.VMEM((N_HEADS, TK, TQ), F32),
                        pltpu.VMEM((N_HEADS, TK, TQ), F32),
                        pltpu.VMEM((N_HEADS, TK, TQ), BF16)],
        compiler_params=pltpu.CompilerParams(dimension_semantics=("arbitrary", "arbitrary")),
        name="attn",
    )(q, k, vt, iq, ikk, iwt)


def _post_kernel(x_ref, a_ref, b_ref, woa_ref, wob_ref, gmix_ref, gffn_ref, w1_ref, w2_ref, gout_ref, o_ref):
    mix = (jnp.dot(a_ref[...], woa_ref[...], preferred_element_type=F32)
           + jnp.dot(b_ref[...], wob_ref[...], preferred_element_type=F32))
    x1 = x_ref[...] + _rms(mix, gmix_ref[...])
    h = _rms(x1, gffn_ref[...]).astype(BF16)
    f = None
    for c in range(D_FF // FF_CHUNK):
        cols = slice(c * FF_CHUNK, (c + 1) * FF_CHUNK)
        f1 = jnp.square(jnp.maximum(jnp.dot(h, w1_ref[:, cols], preferred_element_type=F32), 0.0)).astype(BF16)
        part = jnp.dot(f1, w2_ref[cols, :], preferred_element_type=F32)
        f = part if f is None else f + part
    o_ref[...] = x1 + _rms(f, gout_ref[...])


def _post_call(x2, a, b, woa, wob, gmix, gffn, w1, w2, gout):
    n = x2.shape[0]
    tm = ROW_TILE
    row = lambda width: pl.BlockSpec((tm, width), lambda i: (i, 0))
    gspec = _const_spec((1, D_MODEL))
    return pl.pallas_call(
        _post_kernel,
        grid=(n // tm,),
        in_specs=[row(D_MODEL), row(POOL_WIDTH), row(ATTN_WIDTH), _const_spec(woa.shape), _const_spec(wob.shape),
                  gspec, gspec, _const_spec(w1.shape), _const_spec(w2.shape), gspec],
        out_specs=row(D_MODEL),
        out_shape=jax.ShapeDtypeStruct((n, D_MODEL), F32),
        compiler_params=pltpu.CompilerParams(dimension_semantics=("arbitrary",),
                                             vmem_limit_bytes=56 * 1024 * 1024),
        name="post",
    )(x2, a, b, woa, wob, gmix, gffn, w1, w2, gout)


def _rope_tables(positions):
    inv = ROPE_THETA ** (-jnp.arange(ROPE_HALF, dtype=F32) / ROPE_HALF)
    ang = positions.astype(F32).reshape(-1, 1) * inv
    cos, sin = jnp.cos(ang), jnp.sin(ang)
    rest = HEAD_DIM - 2 * ROPE_HALF
    cos64 = jnp.concatenate([cos, cos, jnp.ones((cos.shape[0], rest), F32)], axis=-1)
    sin64 = jnp.concatenate([-sin, sin, jnp.zeros((sin.shape[0], rest), F32)], axis=-1)
    return jnp.tile(cos64, (1, LANES // HEAD_DIM)), jnp.tile(sin64, (1, LANES // HEAD_DIM))


def kernel(x, positions, g_pre_mix, w_in, w_pool, pool_scale, w_out, g_post_mix, g_pre_ffn, w_ff1, w_ff2,
           g_post_ffn):
    batch, seq, _ = x.shape
    depth = w_in.shape[0]
    assert seq % ROW_TILE == 0 and seq % TQ == 0 and TK % TQ == 0 and seq // 4 >= TOPK
    assert SCORE_ROWS == 1 << CHUNK_SHIFT
    cos_t, sin_t = _rope_tables(positions)
    x2 = x.reshape(batch * seq, D_MODEL)
    c_v = POOL_WIDTH + 2 * ATTN_WIDTH
    c_iq = c_v + ATTN_WIDTH
    c_ik = c_iq + IDX_HEADS * IDX_DIM
    c_iw = c_ik + IDX_DIM
    for l in range(depth):
        wl = w_in[l]
        w_ik = wl[:, c_ik:c_iw]
        w_main = jnp.concatenate([wl[:, :c_v], wl[:, c_iq:c_ik], w_ik, w_ik], axis=1).astype(BF16)
        wvt = wl[:, c_v:c_iq].T.astype(BF16)
        wiwt = jnp.pad(wl[:, c_iw:].T, ((0, IW_ROWS - IDX_HEADS), (0, 0))).astype(BF16)
        a, q, k, vt, iq, ikk, iwt = _proj_call(
            x2, g_pre_mix[l][None], w_main, wvt, wiwt, cos_t, sin_t, w_pool[l].astype(BF16),
            pool_scale[l][None], seq=seq)
        b = _attn_call(q, k, vt, iq, ikk, iwt, batch=batch, seq=seq)
        wo = w_out[l].astype(BF16)
        x2 = _post_call(x2, a, b, wo[:POOL_WIDTH], wo[POOL_WIDTH:], g_post_mix[l][None], g_pre_ffn[l][None],
                        w_ff1[l].astype(BF16), w_ff2[l].astype(BF16), g_post_ffn[l][None])
    return x2.reshape(batch, seq, D_MODEL)
```

```python
import functools

import jax
import jax.numpy as jnp
from jax import lax
from jax.experimental import pallas as pl
from jax.experimental.pallas import tpu as pltpu

F32 = jnp.float32
BF16 = jnp.bfloat16
I32 = jnp.int32
I16 = jnp.int16

D_MODEL = 1024
CHUNK_SHIFT = 6
POOL_WIDTH = 512
POOL_WINDOWS = (2, 4, 8, 16)
POOL_GROUP = 128
POOL_HALO = 16
ATTN_WIDTH = 512
HEAD_DIM = 64
N_HEADS = 8
N_PAIRS = N_HEADS // 2
ROPE_HALF = 8
ROPE_THETA = 500000.0
IDX_HEADS = 8
IDX_DIM = 64
TOPK = 256
D_FF = 4096
EPS = 1e-6
LANES = 128
IW_ROWS = 16

ROW_TILE = 512
TQ = 256
TK = ROW_TILE
SCORE_ROWS = 64
FF_CHUNK = 1024
MASK_NEG = -1e30
LOG2_E = 1.4426950408889634
LOGITS_AHEAD = 2
PACK_ROWS = 16
COUNT_CHAINS = 4
I16_MIN, I16_MAX = -32768, 32767
F32_LOWEST = -3.4028234663852886e38

NT_DIMS = (((1,), (1,)), ((), ()))


def _rms(x, g):
    return x * lax.rsqrt(jnp.mean(x * x, axis=-1, keepdims=True) + EPS) * g


def _tree(fn, xs):
    xs = list(xs)
    while len(xs) > 1:
        nxt = [fn(xs[i], xs[i + 1]) for i in range(0, len(xs) - 1, 2)]
        if len(xs) % 2:
            nxt.append(xs[-1])
        xs = nxt
    return xs[0]


def _rows8(fn, x):
    return _tree(fn, [x[i:i + 8] for i in range(0, x.shape[0], 8)])


def _proj_kernel(x_ref, g_ref, w_ref, wvt_ref, wiwt_ref, cos_ref, sin_ref, wpool_ref, pscale_ref,
                 a_ref, q_ref, k_ref, vt_ref, iq_ref, ikk_ref, iwt_ref, ubuf_ref, *, tiles_per_seq):
    tm = x_ref.shape[0]
    t_in_seq = pl.program_id(0) % tiles_per_seq
    h = _rms(x_ref[...], g_ref[...]).astype(BF16)

    u = jnp.dot(h, w_ref[:, 0:POOL_WIDTH], preferred_element_type=F32)

    @pl.when(t_in_seq == 0)
    def _():
        ubuf_ref[0:POOL_HALO, :] = jnp.zeros((POOL_HALO, POOL_WIDTH), F32)

    @pl.when(t_in_seq != 0)
    def _():
        ubuf_ref[0:POOL_HALO, :] = ubuf_ref[tm:tm + POOL_HALO, :]

    ubuf_ref[POOL_HALO:POOL_HALO + tm, :] = u
    pos = t_in_seq * tm + lax.broadcasted_iota(I32, (tm, 1), 0)

    def pool_group(g):
        w = POOL_WINDOWS[g]
        c0 = g * POOL_GROUP
        ssum = ubuf_ref[:, c0:c0 + POOL_GROUP]
        span = 1
        while span < w:
            ssum = ssum + pltpu.roll(ssum, span, 0)
            span *= 2
        ug = u[:, c0:c0 + POOL_GROUP]
        cnt = jnp.minimum(pos + 1, w).astype(F32)
        d = (ssum[POOL_HALO:, :] / cnt - ug).astype(BF16)
        y = jnp.dot(d, wpool_ref[g], preferred_element_type=F32) * pscale_ref[:, c0:c0 + POOL_GROUP]
        a_ref[:, c0:c0 + POOL_GROUP] = y.astype(BF16)

    cos_t = cos_ref[...]
    sin_t = sin_ref[...]
    first_half = (lax.broadcasted_iota(I32, (tm, LANES), 1) % HEAD_DIM) < ROPE_HALF

    def rope(z):
        partner = jnp.where(first_half, pltpu.roll(z, LANES - ROPE_HALF, 1), pltpu.roll(z, ROPE_HALF, 1))
        return z * cos_t + partner * sin_t

    def head_group(col0, out_ref, scale):
        z = jnp.dot(h, w_ref[:, col0:col0 + ATTN_WIDTH], preferred_element_type=F32)
        for j in range(N_PAIRS):
            r = rope(z[:, j * LANES:(j + 1) * LANES])
            if scale != 1.0:
                r = r * scale
            out_ref[j] = r.astype(BF16)

    head_group(POOL_WIDTH, q_ref, HEAD_DIM ** -0.5 * LOG2_E)
    pool_group(0)
    head_group(POOL_WIDTH + ATTN_WIDTH, k_ref, 1.0)
    pool_group(1)
    head_group(POOL_WIDTH + 2 * ATTN_WIDTH, iq_ref, IDX_DIM ** -0.5)
    pool_group(2)
    col_ik = POOL_WIDTH + 3 * ATTN_WIDTH
    ikk = jnp.dot(h, w_ref[:, col_ik:col_ik + LANES], preferred_element_type=F32)
    ikk_ref[...] = rope(ikk).astype(BF16)

    vt_ref[0] = lax.dot_general(wvt_ref[...], h, NT_DIMS, preferred_element_type=F32).astype(BF16)
    iwt_ref[...] = lax.dot_general(wiwt_ref[...], h, NT_DIMS, preferred_element_type=F32) * (IDX_HEADS ** -0.5)
    pool_group(3)


def _const_spec(shape):
    return pl.BlockSpec(shape, lambda *_: (0,) * len(shape))


def _proj_call(x2, g, w_main, wvt, wiwt, cos_t, sin_t, wpool, pscale, *, seq):
    n = x2.shape[0]
    tm = ROW_TILE
    tps = seq // tm
    row = lambda width: pl.BlockSpec((tm, width), lambda i: (i, 0))
    pair = pl.BlockSpec((N_PAIRS, tm, LANES), lambda i: (0, i, 0))
    pair_shape = jax.ShapeDtypeStruct((N_PAIRS, n, LANES), BF16)
    return pl.pallas_call(
        functools.partial(_proj_kernel, tiles_per_seq=tps),
        grid=(n // tm,),
        in_specs=[row(D_MODEL), _const_spec((1, D_MODEL)), _const_spec(w_main.shape), _const_spec(wvt.shape),
                  _const_spec(wiwt.shape), row(LANES), row(LANES), _const_spec(wpool.shape),
                  _const_spec((1, POOL_WIDTH))],
        out_specs=[row(POOL_WIDTH), pair, pair,
                   pl.BlockSpec((1, ATTN_WIDTH, tm), lambda i: (i, 0, 0)),
                   pair, row(LANES), pl.BlockSpec((IW_ROWS, tm), lambda i: (0, i))],
        out_shape=[jax.ShapeDtypeStruct((n, POOL_WIDTH), BF16), pair_shape, pair_shape,
                   jax.ShapeDtypeStruct((n // tm, ATTN_WIDTH, tm), BF16),
                   pair_shape, jax.ShapeDtypeStruct((n, LANES), BF16),
                   jax.ShapeDtypeStruct((IW_ROWS, n), F32)],
        scratch_shapes=[pltpu.VMEM((POOL_HALO + tm, POOL_WIDTH), F32)],
        compiler_params=pltpu.CompilerParams(dimension_semantics=("arbitrary",)),
        name="proj",
    )(x2, g, w_main, wvt, wiwt, cos_t, sin_t, wpool, pscale)


def _key_of(x):
    b = lax.bitcast_convert_type(x, I32)
    sign = b >> 31
    return (b ^ (sign & 0x7FFFFFFF)) - sign


def _val_of(key):
    k = key + (key >> 31)
    return lax.bitcast_convert_type(k ^ ((k >> 31) & 0x7FFFFFFF), F32)


def _attn_kernel(q_ref, k_ref, vt_ref, iq_ref, ikk_ref, iwt_ref, o_ref,
                 qs_ref, iqs_ref, st_ref, khi_ref, klo_ref, lg_ref, m_ref, ls_ref, ot_ref,
                 s_ref, t_ref, p_ref):
    tq = q_ref.shape[1]
    tk = vt_ref.shape[2]
    q0 = pl.program_id(1) * tq
    n_kt = (q0 + tq + tk - 1) // tk
    q_chunk = (q0 + lax.broadcasted_iota(I32, (1, tq), 1)) >> CHUNK_SHIFT

    low = lax.broadcasted_iota(I32, (tq, LANES), 1) < HEAD_DIM
    for j in range(N_PAIRS):
        qp = q_ref[j].astype(F32)
        qs_ref[2 * j] = jnp.where(low, qp, 0.0).astype(BF16)
        qs_ref[2 * j + 1] = jnp.where(low, 0.0, qp).astype(BF16)
        ip = iq_ref[j].astype(F32)
        iqs_ref[2 * j * tq:(2 * j + 1) * tq, :] = jnp.where(low, ip, 0.0).astype(BF16)
        iqs_ref[(2 * j + 1) * tq:(2 * j + 2) * tq, :] = jnp.where(low, 0.0, ip).astype(BF16)

    def stage_logits(kt, hh):
        r0 = pl.multiple_of(kt * tk, tk)
        s_ref[hh] = lax.dot_general(k_ref[hh // 2, pl.ds(r0, tk), :], qs_ref[hh], NT_DIMS,
                                    preferred_element_type=F32)

    def score_tile(kt, masked):
        r0 = pl.multiple_of(kt * tk, tk)
        lg_ref[...] = lax.dot_general(ikk_ref[pl.ds(r0, tk), :], iqs_ref[...], NT_DIMS,
                                      preferred_element_type=F32)
        if masked:
            for hh in range(N_HEADS):
                stage_logits(0, hh)
        for r in range(tk // SCORE_ROWS):
            rows = slice(r * SCORE_ROWS, (r + 1) * SCORE_ROWS)
            sc = None
            for hh in range(IDX_HEADS):
                c = iwt_ref[hh:hh + 1, :] * jnp.maximum(lg_ref[rows, hh * tq:(hh + 1) * tq], 0.0)
                sc = c if sc is None else sc + c
            if masked:
                allowed = ((r0 + r * SCORE_ROWS) >> CHUNK_SHIFT) <= q_chunk
                sc = jnp.where(allowed, sc, -jnp.inf)
            out_rows = pl.ds(r0 + r * SCORE_ROWS, SCORE_ROWS)
            st_ref[out_rows, :] = sc
            key = _key_of(sc)
            khi_ref[out_rows, :] = (key >> 16).astype(I16)
            klo_ref[out_rows, :] = (key ^ 0x8000).astype(I16)

    def score_body(kt, c):
        score_tile(kt, False)
        return c

    lax.fori_loop(0, n_kt - 1, score_body, 0)
    score_tile(n_kt - 1, True)

    def count16(src_ref, cand):
        c16 = cand.astype(I16)

        def body(kt, accs):
            blk = src_ref[pl.ds(pl.multiple_of(kt * tk, tk), tk), :]
            ind = jnp.where(blk >= c16, jnp.int16(1), jnp.int16(0))
            parts = [ind[i:i + PACK_ROWS] for i in range(0, tk, PACK_ROWS)]
            return tuple(a + _tree(jnp.add, parts[j::len(accs)]) for j, a in enumerate(accs))

        zero = jnp.zeros((PACK_ROWS, tq), I16)
        accs = lax.fori_loop(0, n_kt, body, (zero,) * COUNT_CHAINS)
        return jnp.sum(_tree(jnp.add, list(accs)).astype(I32), axis=0, keepdims=True)

    n_allowed = (q_chunk + 1) << CHUNK_SHIFT
    searched = n_allowed > TOPK
    n_stored = jnp.full((1, tq), n_kt * tk, I32)

    def radix_step(src_ref, need, bit, state):
        t, c_t, c_refused = state
        cand = t + jnp.left_shift(jnp.int32(1), bit)
        c = count16(src_ref, cand)
        ok = c >= need
        return jnp.where(ok, cand, t), jnp.where(ok, c, c_t), jnp.where(ok, c_refused, c)

    start = (jnp.full((1, tq), I16_MIN, I32), n_stored, jnp.zeros((1, tq), I32))
    t_hi, _, c_above = lax.fori_loop(0, 16, lambda i, st: radix_step(khi_ref, TOPK, 15 - i, st), start)
    t_hi16 = t_hi.astype(I16)

    def keep_bucket(kt, c):
        rows = pl.ds(pl.multiple_of(kt * tk, tk), tk)
        klo_ref[rows, :] = jnp.where(khi_ref[rows, :] == t_hi16, klo_ref[rows, :], jnp.int16(I16_MIN))
        return c

    lax.fori_loop(0, n_kt, keep_bucket, 0)

    t_lo, c_bucket, _ = lax.fori_loop(
        0, 16, lambda i, st: radix_step(klo_ref, TOPK - c_above, 15 - i, st), start)
    thr_key = (t_hi << 16) | (t_lo + 0x8000)
    thr = jnp.where(searched, _val_of(thr_key), F32_LOWEST)
    c_lo = jnp.where(searched, c_above + c_bucket, n_allowed).astype(F32)

    has_tie = jnp.max(jnp.where(c_lo > TOPK, 1.0, 0.0)) > 0.5

    @pl.when(has_tie)
    def _():
        def count_above(kt, acc):
            blk = st_ref[pl.ds(pl.multiple_of(kt * tk, tk), tk), :]
            return acc + _rows8(jnp.add, jnp.where(blk > thr, 1.0, 0.0))
        above = lax.fori_loop(0, n_kt, count_above, jnp.zeros((8, tq), F32))
        budget = TOPK - jnp.sum(above, axis=0, keepdims=True)
        ri = lax.broadcasted_iota(I32, (tk, tk), 0)
        ci = lax.broadcasted_iota(I32, (tk, tk), 1)
        prefix = jnp.where(ci <= ri, 1.0, 0.0).astype(BF16)

        def body(kt, seen):
            rows = pl.ds(pl.multiple_of(kt * tk, tk), tk)
            blk = st_ref[rows, :]
            eq = blk == thr
            rank = jnp.dot(prefix, jnp.where(eq, 1.0, 0.0).astype(BF16), preferred_element_type=F32) + seen
            keep = (blk > thr) | (eq & (rank <= budget))
            st_ref[rows, :] = jnp.where(keep, blk, -jnp.inf)
            return rank[tk - 1:tk, :]
        lax.fori_loop(0, n_kt, body, jnp.zeros((1, tq), F32))

    m_ref[...] = jnp.full(m_ref.shape, -jnp.inf, F32)
    ls_ref[...] = jnp.zeros(ls_ref.shape, F32)
    ot_ref[...] = jnp.zeros(ot_ref.shape, F32)
    n_sub = tk // SCORE_ROWS
    ones_rows = jnp.ones((PACK_ROWS, tk), BF16)

    def kt_body(kt, c):
        r0 = pl.multiple_of(kt * tk, tk)
        kt_next = jnp.minimum(kt + 1, n_kt - 1)

        mx = [None] * N_HEADS
        for r in range(n_sub):
            rows = slice(r * SCORE_ROWS, (r + 1) * SCORE_ROWS)
            bias = jnp.where(st_ref[pl.ds(r0 + r * SCORE_ROWS, SCORE_ROWS), :] >= thr, 0.0, MASK_NEG)
            for hh in range(N_HEADS):
                t = s_ref[hh, rows, :] + bias
                t_ref[hh, rows, :] = t
                m8 = _rows8(jnp.maximum, t)
                mx[hh] = m8 if mx[hh] is None else jnp.maximum(mx[hh], m8)

        for hh in range(LOGITS_AHEAD):
            stage_logits(kt_next, hh)
        for hh in range(N_HEADS):
            feat = slice(hh * HEAD_DIM, (hh + 1) * HEAD_DIM)
            m_prev = m_ref[hh]
            m_new = jnp.maximum(m_prev, jnp.max(mx[hh], axis=0, keepdims=True))
            m_ref[hh] = m_new
            alpha = jnp.exp2(m_prev - m_new)
            for r in range(n_sub):
                rows = slice(r * SCORE_ROWS, (r + 1) * SCORE_ROWS)
                p_ref[hh, rows, :] = jnp.exp2(t_ref[hh, rows, :] - m_new).astype(BF16)
            v_ext = jnp.concatenate([vt_ref[kt, feat, :], ones_rows], axis=0)
            pv = jnp.dot(v_ext, p_ref[hh], preferred_element_type=F32)
            ls_ref[hh] = alpha * ls_ref[hh] + pv[HEAD_DIM:HEAD_DIM + 1, :]
            ot_ref[feat, :] = alpha * ot_ref[feat, :] + pv[0:HEAD_DIM, :]
            if hh + LOGITS_AHEAD < N_HEADS:
                stage_logits(kt_next, hh + LOGITS_AHEAD)
        return c

    lax.fori_loop(0, n_kt, kt_body, 0)
    for hh in range(N_HEADS):
        feat = slice(hh * HEAD_DIM, (hh + 1) * HEAD_DIM)
        ot_ref[feat, :] = ot_ref[feat, :] / ls_ref[hh]
    o_ref[...] = ot_ref[...].T.astype(BF16)


def _attn_call(q, k, vt, iq, ikk, iwt, *, batch, seq):
    n = batch * seq
    nq = seq // TQ
    n_vt = seq // TK
    qspec = pl.BlockSpec((N_PAIRS, TQ, LANES), lambda b, i: (0, b * nq + i, 0))
    return pl.pallas_call(
        _attn_kernel,
        grid=(batch, nq),
        in_specs=[qspec,
                  pl.BlockSpec((N_PAIRS, seq, LANES), lambda b, i: (0, b, 0)),
                  pl.BlockSpec((n_vt, ATTN_WIDTH, TK), lambda b, i: (b, 0, 0)),
                  qspec,
                  pl.BlockSpec((seq, LANES), lambda b, i: (b, 0)),
                  pl.BlockSpec((IW_ROWS, TQ), lambda b, i: (0, b * nq + i))],
        out_specs=pl.BlockSpec((TQ, ATTN_WIDTH), lambda b, i: (b * nq + i, 0)),
        out_shape=jax.ShapeDtypeStruct((n, ATTN_WIDTH), BF16),
        scratch_shapes=[pltpu.VMEM((N_HEADS, TQ, LANES), BF16),
                        pltpu.VMEM((IDX_HEADS * TQ, LANES), BF16),
                        pltpu.VMEM((seq, TQ), F32),
                        pltpu.VMEM((seq, TQ), I16), pltpu.VMEM((seq, TQ), I16),
                        pltpu.VMEM((TK, IDX_HEADS * TQ), F32),
                        pltpu.VMEM((N_HEADS, 1, TQ), F32), pltpu.VMEM((N_HEADS, 1, TQ), F32),
                        pltpu.VMEM((ATTN_WIDTH, TQ), F32),
                        pltpu.VMEM((N_HEADS, TK, TQ), F32),
                        pltpu.VMEM((N_HEADS, TK, TQ), F32),
                        pltpu.VMEM((N_HEADS, TK, TQ), BF16)],
        compiler_params=pltpu.CompilerParams(dimension_semantics=("arbitrary", "arbitrary")),
        name="attn",
    )(q, k, vt, iq, ikk, iwt)


def _post_kernel(x_ref, a_ref, b_ref, woa_ref, wob_ref, gmix_ref, gffn_ref, w1_ref, w2_ref, gout_ref, o_ref):
    mix = (jnp.dot(a_ref[...], woa_ref[...], preferred_element_type=F32)
           + jnp.dot(b_ref[...], wob_ref[...], preferred_element_type=F32))
    x1 = x_ref[...] + _rms(mix, gmix_ref[...])
    h = _rms(x1, gffn_ref[...]).astype(BF16)
    f = None
    for c in range(D_FF // FF_CHUNK):
        cols = slice(c * FF_CHUNK, (c + 1) * FF_CHUNK)
        f1 = jnp.square(jnp.maximum(jnp.dot(h, w1_ref[:, cols], preferred_element_type=F32), 0.0)).astype(BF16)
        part = jnp.dot(f1, w2_ref[cols, :], preferred_element_type=F32)
        f = part if f is None else f + part
    o_ref[...] = x1 + _rms(f, gout_ref[...])


def _post_call(x2, a, b, woa, wob, gmix, gffn, w1, w2, gout):
    n = x2.shape[0]
    tm = ROW_TILE
    row = lambda width: pl.BlockSpec((tm, width), lambda i: (i, 0))
    gspec = _const_spec((1, D_MODEL))
    return pl.pallas_call(
        _post_kernel,
        grid=(n // tm,),
        in_specs=[row(D_MODEL), row(POOL_WIDTH), row(ATTN_WIDTH), _const_spec(woa.shape), _const_spec(wob.shape),
                  gspec, gspec, _const_spec(w1.shape), _const_spec(w2.shape), gspec],
        out_specs=row(D_MODEL),
        out_shape=jax.ShapeDtypeStruct((n, D_MODEL), F32),
        compiler_params=pltpu.CompilerParams(dimension_semantics=("arbitrary",),
                                             vmem_limit_bytes=56 * 1024 * 1024),
        name="post",
    )(x2, a, b, woa, wob, gmix, gffn, w1, w2, gout)


def _rope_tables(positions):
    inv = ROPE_THETA ** (-jnp.arange(ROPE_HALF, dtype=F32) / ROPE_HALF)
    ang = positions.astype(F32).reshape(-1, 1) * inv
    cos, sin = jnp.cos(ang), jnp.sin(ang)
    rest = HEAD_DIM - 2 * ROPE_HALF
    cos64 = jnp.concatenate([cos, cos, jnp.ones((cos.shape[0], rest), F32)], axis=-1)
    sin64 = jnp.concatenate([-sin, sin, jnp.zeros((sin.shape[0], rest), F32)], axis=-1)
    return jnp.tile(cos64, (1, LANES // HEAD_DIM)), jnp.tile(sin64, (1, LANES // HEAD_DIM))


def kernel(x, positions, g_pre_mix, w_in, w_pool, pool_scale, w_out, g_post_mix, g_pre_ffn, w_ff1, w_ff2,
           g_post_ffn):
    batch, seq, _ = x.shape
    depth = w_in.shape[0]
    assert seq % ROW_TILE == 0 and seq % TQ == 0 and TK % TQ == 0 and seq // 4 >= TOPK
    assert SCORE_ROWS == 1 << CHUNK_SHIFT
    cos_t, sin_t = _rope_tables(positions)
    x2 = x.reshape(batch * seq, D_MODEL)
    c_v = POOL_WIDTH + 2 * ATTN_WIDTH
    c_iq = c_v + ATTN_WIDTH
    c_ik = c_iq + IDX_HEADS * IDX_DIM
    c_iw = c_ik + IDX_DIM
    for l in range(depth):
        wl = w_in[l]
        w_ik = wl[:, c_ik:c_iw]
        w_main = jnp.concatenate([wl[:, :c_v], wl[:, c_iq:c_ik], w_ik, w_ik], axis=1).astype(BF16)
        wvt = wl[:, c_v:c_iq].T.astype(BF16)
        wiwt = jnp.pad(wl[:, c_iw:].T, ((0, IW_ROWS - IDX_HEADS), (0, 0))).astype(BF16)
        a, q, k, vt, iq, ikk, iwt = _proj_call(
            x2, g_pre_mix[l][None], w_main, wvt, wiwt, cos_t, sin_t, w_pool[l].astype(BF16),
            pool_scale[l][None], seq=seq)
        b = _attn_call(q, k, vt, iq, ikk, iwt, batch=batch, seq=seq)
        wo = w_out[l].astype(BF16)
        x2 = _post_call(x2, a, b, wo[:POOL_WIDTH], wo[POOL_WIDTH:], g_post_mix[l][None], g_pre_ffn[l][None],
                        w_ff1[l].astype(BF16), w_ff2[l].astype(BF16), g_post_ffn[l][None])
    return x2.reshape(batch, seq, D_MODEL)
```
